```python
import jax, jax.numpy as jnp
from jax import lax
import numpy as np

D_MODEL = 2048
BATCH = 2
SEQ = 4096
DEPTH = 1
DEC_BATCH = 4
DEC_SEQ = 4096
PAST_LEN = 128

D_FF = 5504
MEM_LEN = 256
MEM_HEADS = 4
MEM_DH = D_MODEL // MEM_HEADS
GLA_HEADS = 4
GLA_DK = 128
GLA_DV = 256
GLA_GATE_RANK = 16
GLA_GATE_TEMP = 16.0
GLA_CHUNK = 64
MLA_HEADS = 8
MLA_Q_RANK = 512
MLA_KV_RANK = 256
MLA_NOPE = 128
MLA_ROPE = 64
MLA_DV = 128
ROPE_THETA = 10000.0
Q_BLOCK = 128
EPS = 1e-6

GLA_QK = GLA_HEADS * GLA_DK
GLA_V = GLA_HEADS * GLA_DV
MLA_V = MLA_HEADS * MLA_DV
MIX_WIDTH = GLA_V + MLA_V
MLA_QK_HEAD = MLA_NOPE + MLA_ROPE
IN_SPLITS = (GLA_QK, GLA_QK, GLA_V, GLA_V, GLA_GATE_RANK, GLA_GATE_RANK, MLA_Q_RANK, MLA_KV_RANK, MLA_ROPE)
D_IN = 2 * GLA_QK + 2 * GLA_V + 2 * GLA_GATE_RANK + MLA_Q_RANK + MLA_KV_RANK + MLA_ROPE

kernel_name = "hybrid_gla_mla_macaron_encoder"


def rms_norm(x, g):
    xf = x.astype(jnp.float32)
    y = xf * lax.rsqrt(jnp.mean(xf * xf, axis=-1, keepdims=True) + EPS)
    return (y * g.astype(jnp.float32)).astype(x.dtype)


def swiglu(x, w_gate, w_up, w_down):
    return (jax.nn.silu(x @ w_gate) * (x @ w_up)) @ w_down


def split_in(z):
    parts, start = [], 0
    for width in IN_SPLITS:
        parts.append(z[..., start:start + width])
        start += width
    return parts


def rotary(x, pos):
    half = x.shape[-1] // 2
    inv = ROPE_THETA ** (-jnp.arange(half, dtype=jnp.float32) / half)
    ang = pos.astype(jnp.float32)[:, None] * inv[None, :]
    cos = jnp.cos(ang)[None, :, None, :]
    sin = jnp.sin(ang)[None, :, None, :]
    xf = x.astype(jnp.float32)
    x1, x2 = xf[..., :half], xf[..., half:]
    return jnp.concatenate([x1 * cos - x2 * sin, x2 * cos + x1 * sin], axis=-1).astype(x.dtype)


def gla_scan(q, k, v, log_a):
    B, S, H, dk = q.shape
    dv = v.shape[-1]
    C = GLA_CHUNK
    N = S // C
    q = q.astype(jnp.float32).reshape(B, N, C, H, dk)
    k = k.astype(jnp.float32).reshape(B, N, C, H, dk)
    v = v.astype(jnp.float32).reshape(B, N, C, H, dv)
    b = jnp.cumsum(log_a.astype(jnp.float32).reshape(B, N, C, H, dk), axis=2)
    b_last = b[:, :, -1]
    q_t = q * jnp.exp(b)
    k_t = k * jnp.exp(-b)
    k_s = k * jnp.exp(b_last[:, :, None] - b)
    A = jnp.einsum('bnihd,bnjhd->bnhij', q_t, k_t)
    mask = jnp.tril(jnp.ones((C, C), dtype=bool))
    A = jnp.where(mask, A, 0.0)
    o_intra = jnp.einsum('bnhij,bnjhv->bnihv', A, v)

    def step(state, xs):
        qt_n, ks_n, v_n, dec_n = xs
        o_n = jnp.einsum('bihd,bhdv->bihv', qt_n, state)
        state = jnp.exp(dec_n)[..., None] * state + jnp.einsum('bjhd,bjhv->bhdv', ks_n, v_n)
        return state, o_n

    xs = (jnp.moveaxis(q_t, 1, 0), jnp.moveaxis(k_s, 1, 0), jnp.moveaxis(v, 1, 0), jnp.moveaxis(b_last, 1, 0))
    init = jnp.zeros((B, H, dk, dv), jnp.float32)
    _, o_inter = lax.scan(step, init, xs)
    o = o_intra + jnp.moveaxis(o_inter, 0, 1)
    return o.reshape(B, S, H, dv)


def gla_group(gq, gk, gv, gr, ga_f, ga_b, wa2_f, ba_f, wa2_b, ba_b, out_norm):
    B, S, _ = gq.shape
    q = gq.reshape(B, S, GLA_HEADS, GLA_DK) * (GLA_DK ** -0.5)
    k = gk.reshape(B, S, GLA_HEADS, GLA_DK)
    v = gv.reshape(B, S, GLA_HEADS, GLA_DV)

    def log_gate(c, w2, bias):
        logits = (c @ w2 + bias).astype(jnp.float32)
        return (jax.nn.log_sigmoid(logits) / GLA_GATE_TEMP).reshape(B, S, GLA_HEADS, GLA_DK)

    flip = lambda t: jnp.flip(t, axis=1)
    o_f = gla_scan(q, k, v, log_gate(ga_f, wa2_f, ba_f))
    o_b = flip(gla_scan(flip(q), flip(k), flip(v), flip(log_gate(ga_b, wa2_b, ba_b))))
    o = o_f + o_b
    o = o * lax.rsqrt(jnp.mean(o * o, axis=-1, keepdims=True) + EPS)
    o = o * out_norm.astype(jnp.float32).reshape(GLA_HEADS, GLA_DV)
    return o.reshape(B, S, GLA_V).astype(gr.dtype) * jax.nn.silu(gr)


def block_attention(q, k, v):
    B, S, H, dh = q.shape
    dv = v.shape[-1]
    nb = S // Q_BLOCK
    scale = dh ** -0.5
    qb = q.reshape(B, nb, Q_BLOCK, H, dh).transpose(1, 0, 2, 3, 4)

    def one(qi):
        s = jnp.einsum('bqhd,bkhd->bhqk', qi, k).astype(jnp.float32) * scale
        p = jax.nn.softmax(s, axis=-1).astype(v.dtype)
        return jnp.einsum('bhqk,bkhv->bqhv', p, v)

    o = lax.map(one, qb)
    return o.transpose(1, 0, 2, 3, 4).reshape(B, S, H, dv)


def mla_group(cq, ckv, kpe, q_norm, w_uq, kv_norm, w_ukv, qk_q_norm, qk_k_norm, pos):
    B, S, _ = cq.shape
    q = (rms_norm(cq, q_norm) @ w_uq).reshape(B, S, MLA_HEADS, MLA_QK_HEAD)
    kv = (rms_norm(ckv, kv_norm) @ w_ukv).reshape(B, S, MLA_HEADS, MLA_NOPE + MLA_DV)
    k_nope, v = kv[..., :MLA_NOPE], kv[..., MLA_NOPE:]
    k_rope = jnp.broadcast_to(kpe[:, :, None, :], (B, S, MLA_HEADS, MLA_ROPE))
    k = jnp.concatenate([k_nope, k_rope], axis=-1)
    q = rms_norm(q, qk_q_norm)
    k = rms_norm(k, qk_k_norm)
    q = jnp.concatenate([q[..., :MLA_NOPE], rotary(q[..., MLA_NOPE:], pos)], axis=-1)
    k = jnp.concatenate([k[..., :MLA_NOPE], rotary(k[..., MLA_NOPE:], pos)], axis=-1)
    return block_attention(q, k, v).reshape(B, S, MLA_V)


def memory_xattn(xn, memn, wq, wk, wv, qn, kn, wo):
    B, S, _ = xn.shape
    M = memn.shape[1]
    q = rms_norm((xn @ wq).reshape(B, S, MEM_HEADS, MEM_DH), qn)
    k = rms_norm((memn @ wk).reshape(B, M, MEM_HEADS, MEM_DH), kn)
    v = (memn @ wv).reshape(B, M, MEM_HEADS, MEM_DH)
    s = jnp.einsum('bqhd,bkhd->bhqk', q, k).astype(jnp.float32) * (MEM_DH ** -0.5)
    p = jax.nn.softmax(s, axis=-1).astype(v.dtype)
    o = jnp.einsum('bhqk,bkhv->bqhv', p, v).reshape(B, S, D_MODEL)
    return o @ wo


def encoder_layer(x, mem, pos, P, l):
    h = x + 0.5 * swiglu(rms_norm(x, P['ffn1_norm'][l]), P['ffn1_w_gate'][l], P['ffn1_w_up'][l], P['ffn1_w_down'][l])
    n = rms_norm(h, P['mix_norm'][l])
    z = n @ P['w_in'][l]
    gq, gk, gv, gr, ga_f, ga_b, cq, ckv, kpe = split_in(z)
    y_gla = gla_group(gq, gk, gv, gr, ga_f, ga_b,
                      P['gla_wa2_fwd'][l], P['gla_ba_fwd'][l], P['gla_wa2_bwd'][l], P['gla_ba_bwd'][l],
                      P['gla_out_norm'][l])
    y_mla = mla_group(cq, ckv, kpe, P['mla_q_norm'][l], P['mla_w_uq'][l], P['mla_kv_norm'][l],
                      P['mla_w_ukv'][l], P['mla_qk_q_norm'][l], P['mla_qk_k_norm'][l], pos)
    h = h + jnp.concatenate([y_gla, y_mla], axis=-1) @ P['w_out'][l]
    h = h + memory_xattn(rms_norm(h, P['xattn_norm'][l]), rms_norm(mem, P['mem_norm'][l]),
                         P['xattn_wq'][l], P['xattn_wk'][l], P['xattn_wv'][l],
                         P['xattn_q_norm'][l], P['xattn_k_norm'][l], P['xattn_wo'][l])
    h = h + 0.5 * swiglu(rms_norm(h, P['ffn2_norm'][l]), P['ffn2_w_gate'][l], P['ffn2_w_up'][l], P['ffn2_w_down'][l])
    return h


def run_trunk(x, mem, P):
    pos = jnp.arange(x.shape[1], dtype=jnp.int32)
    h = x
    for l in range(DEPTH):
        h = encoder_layer(h, mem, pos, P, l)
    return h


def setup_inputs(seed: int = 0) -> dict:
    key = jax.random.key(seed)
    ks = iter(jax.random.split(key, 64))

    def w(shape, fan_in):
        return jax.random.normal(next(ks), shape, jnp.float32) * (fan_in ** -0.5)

    def gain(shape):
        return 1.0 + 0.02 * jax.random.normal(next(ks), shape, jnp.float32)

    def bias(shape):
        return 0.1 * jax.random.normal(next(ks), shape, jnp.float32)

    L = DEPTH
    return {
        "x_prompt": jax.random.normal(next(ks), (BATCH, SEQ, D_MODEL), jnp.float32),
        "x_sample": jax.random.normal(next(ks), (DEC_BATCH, DEC_SEQ, D_MODEL), jnp.float32),
        "mem_prompt": jax.random.normal(next(ks), (BATCH, MEM_LEN, D_MODEL), jnp.float32),
        "mem_sample": jax.random.normal(next(ks), (DEC_BATCH, MEM_LEN, D_MODEL), jnp.float32),
        "ffn1_norm": gain((L, D_MODEL)),
        "ffn1_w_gate": w((L, D_MODEL, D_FF), D_MODEL),
        "ffn1_w_up": w((L, D_MODEL, D_FF), D_MODEL),
        "ffn1_w_down": w((L, D_FF, D_MODEL), D_FF),
        "mix_norm": gain((L, D_MODEL)),
        "w_in": w((L, D_MODEL, D_IN), D_MODEL),
        "gla_wa2_fwd": w((L, GLA_GATE_RANK, GLA_QK), GLA_GATE_RANK),
        "gla_ba_fwd": bias((L, GLA_QK)),
        "gla_wa2_bwd": w((L, GLA_GATE_RANK, GLA_QK), GLA_GATE_RANK),
        "gla_ba_bwd": bias((L, GLA_QK)),
        "gla_out_norm": gain((L, GLA_V)),
        "mla_q_norm": gain((L, MLA_Q_RANK)),
        "mla_w_uq": w((L, MLA_Q_RANK, MLA_HEADS * MLA_QK_HEAD), MLA_Q_RANK),
        "mla_kv_norm": gain((L, MLA_KV_RANK)),
        "mla_w_ukv": w((L, MLA_KV_RANK, MLA_HEADS * (MLA_NOPE + MLA_DV)), MLA_KV_RANK),
        "mla_qk_q_norm": gain((L, MLA_QK_HEAD)),
        "mla_qk_k_norm": gain((L, MLA_QK_HEAD)),
        "w_out": w((L, MIX_WIDTH, D_MODEL), MIX_WIDTH),
        "xattn_norm": gain((L, D_MODEL)),
        "mem_norm": gain((L, D_MODEL)),
        "xattn_wq": w((L, D_MODEL, D_MODEL), D_MODEL),
        "xattn_wk": w((L, D_MODEL, D_MODEL), D_MODEL),
        "xattn_wv": w((L, D_MODEL, D_MODEL), D_MODEL),
        "xattn_q_norm": gain((L, MEM_DH)),
        "xattn_k_norm": gain((L, MEM_DH)),
        "xattn_wo": w((L, D_MODEL, D_MODEL), D_MODEL),
        "ffn2_norm": gain((L, D_MODEL)),
        "ffn2_w_gate": w((L, D_MODEL, D_FF), D_MODEL),
        "ffn2_w_up": w((L, D_MODEL, D_FF), D_MODEL),
        "ffn2_w_down": w((L, D_FF, D_MODEL), D_FF),
    }


def reference(x_prompt, x_sample, mem_prompt, mem_sample,
              ffn1_norm, ffn1_w_gate, ffn1_w_up, ffn1_w_down,
              mix_norm, w_in,
              gla_wa2_fwd, gla_ba_fwd, gla_wa2_bwd, gla_ba_bwd, gla_out_norm,
              mla_q_norm, mla_w_uq, mla_kv_norm, mla_w_ukv, mla_qk_q_norm, mla_qk_k_norm,
              w_out,
              xattn_norm, mem_norm, xattn_wq, xattn_wk, xattn_wv, xattn_q_norm, xattn_k_norm, xattn_wo,
              ffn2_norm, ffn2_w_gate, ffn2_w_up, ffn2_w_down):
    P = dict(
        ffn1_norm=ffn1_norm, ffn1_w_gate=ffn1_w_gate, ffn1_w_up=ffn1_w_up, ffn1_w_down=ffn1_w_down,
        mix_norm=mix_norm, w_in=w_in,
        gla_wa2_fwd=gla_wa2_fwd, gla_ba_fwd=gla_ba_fwd, gla_wa2_bwd=gla_wa2_bwd, gla_ba_bwd=gla_ba_bwd,
        gla_out_norm=gla_out_norm,
        mla_q_norm=mla_q_norm, mla_w_uq=mla_w_uq, mla_kv_norm=mla_kv_norm, mla_w_ukv=mla_w_ukv,
        mla_qk_q_norm=mla_qk_q_norm, mla_qk_k_norm=mla_qk_k_norm,
        w_out=w_out,
        xattn_norm=xattn_norm, mem_norm=mem_norm, xattn_wq=xattn_wq, xattn_wk=xattn_wk, xattn_wv=xattn_wv,
        xattn_q_norm=xattn_q_norm, xattn_k_norm=xattn_k_norm, xattn_wo=xattn_wo,
        ffn2_norm=ffn2_norm, ffn2_w_gate=ffn2_w_gate, ffn2_w_up=ffn2_w_up, ffn2_w_down=ffn2_w_down,
    )
    y_prompt = run_trunk(x_prompt, mem_prompt, P)
    y_sample = run_trunk(x_sample, mem_sample, P)
    return (y_prompt, y_sample)
```

```python
import functools

import jax
import jax.numpy as jnp
from jax import lax
from jax.experimental import pallas as pl
from jax.experimental.pallas import tpu as pltpu

F32 = jnp.float32
BF16 = jnp.bfloat16

D_MODEL = 2048
D_FF = 5504
MEM_HEADS = 4
MEM_DH = D_MODEL // MEM_HEADS
GLA_HEADS = 4
GLA_DK = 128
GLA_DV = 256
GLA_GATE_RANK = 16
GLA_GATE_TEMP = 16.0
GLA_CHUNK = 64
MLA_HEADS = 8
MLA_Q_RANK = 512
MLA_KV_RANK = 256
MLA_NOPE = 128
MLA_ROPE = 64
MLA_DV = 128
ROPE_THETA = 10000.0
EPS = 1e-6

GLA_QK = GLA_HEADS * GLA_DK
GLA_V = GLA_HEADS * GLA_DV
MLA_V = MLA_HEADS * MLA_DV
MLA_QK_HEAD = MLA_NOPE + MLA_ROPE

LANE = 128
MXU_DIM = 256
MLA_HEAD_PAD = MXU_DIM
D_FF_TILE = 512
D_FF_PAD = -(-D_FF // D_FF_TILE) * D_FF_TILE
Z_GATE_START = 2 * GLA_QK + 2 * GLA_V
Z_MAIN = Z_GATE_START + MLA_Q_RANK + MLA_KV_RANK + MLA_ROPE
Z_TILE = 1024
Z_PAD = -(-Z_MAIN // Z_TILE) * Z_TILE
Z_CQ = Z_GATE_START
Z_CKV = Z_CQ + MLA_Q_RANK
Z_KPE = Z_CKV + MLA_KV_RANK

VMEM_LIMIT = 56 * 1024 * 1024


def _params(*sem):
    return pltpu.CompilerParams(dimension_semantics=sem, vmem_limit_bytes=VMEM_LIMIT)


def _rms(x, g):
    return x * lax.rsqrt(jnp.mean(x * x, axis=-1, keepdims=True) + EPS) * g


def _dot(a, b):
    return jnp.dot(a, b, preferred_element_type=F32)


def _dot_nt(a, b):
    return lax.dot_general(a, b, (((1,), (1,)), ((), ())), preferred_element_type=F32)


def _dot_tn(a, b):
    return lax.dot_general(a, b, (((0,), (0,)), ((), ())), preferred_element_type=F32)


def _ffn_kernel(x_ref, g_ref, wg_ref, wu_ref, wd_ref, o_ref, xn_ref, acc_ref):
    j = pl.program_id(1)

    @pl.when(j == 0)
    def _():
        xn_ref[...] = _rms(x_ref[...], g_ref[...]).astype(BF16)
        acc_ref[...] = jnp.zeros_like(acc_ref)

    xn = xn_ref[...]
    gate = _dot(xn, wg_ref[...])
    up = _dot(xn, wu_ref[...])
    hid = (gate * jax.nn.sigmoid(gate) * up).astype(BF16)
    acc_ref[...] += _dot(hid, wd_ref[...])

    @pl.when(j == pl.num_programs(1) - 1)
    def _():
        o_ref[...] = x_ref[...] + 0.5 * acc_ref[...]


def _ffn(x, g, wg, wu, wd, tm):
    t, d = x.shape
    f = wg.shape[1]
    tf = D_FF_TILE
    return pl.pallas_call(
        _ffn_kernel,
        grid=(t // tm, f // tf),
        in_specs=[
            pl.BlockSpec((tm, d), lambda i, j: (i, 0)),
            pl.BlockSpec((1, d), lambda i, j: (0, 0)),
            pl.BlockSpec((d, tf), lambda i, j: (0, j)),
            pl.BlockSpec((d, tf), lambda i, j: (0, j)),
            pl.BlockSpec((tf, d), lambda i, j: (j, 0)),
        ],
        out_specs=pl.BlockSpec((tm, d), lambda i, j: (i, 0)),
        out_shape=jax.ShapeDtypeStruct((t, d), F32),
        scratch_shapes=[pltpu.VMEM((tm, d), BF16), pltpu.VMEM((tm, d), F32)],
        compiler_params=_params("parallel", "arbitrary"),
        name="ffn",
    )(x, g, wg, wu, wd)


def _log_sigmoid(x):
    return jnp.minimum(x, 0.0) - jnp.log(1.0 + jnp.exp(-jnp.abs(x)))


def _mixin_kernel(x_ref, g_ref, w_ref, wga_ref, w2f_ref, bf_ref, w2b_ref, bb_ref,
                  z_ref, laf_ref, lab_ref, xn_ref):
    @pl.when(pl.program_id(1) == 0)
    def _():
        xn = _rms(x_ref[...], g_ref[...]).astype(BF16)
        xn_ref[...] = xn
        c = _dot(xn, wga_ref[...]).astype(BF16)
        laf_ref[...] = _log_sigmoid(_dot(c, w2f_ref[...]) + bf_ref[...]) * (1.0 / GLA_GATE_TEMP)
        lab_ref[...] = _log_sigmoid(_dot(c, w2b_ref[...]) + bb_ref[...]) * (1.0 / GLA_GATE_TEMP)

    z_ref[...] = _dot(xn_ref[...], w_ref[...]).astype(z_ref.dtype)


def _mixin(x, g, w, wga, w2f, bf, w2b, bb, tm):
    t, d = x.shape
    n = w.shape[1]
    tn = Z_TILE
    const = lambda i, j: (0, 0)
    return pl.pallas_call(
        _mixin_kernel,
        grid=(t // tm, n // tn),
        in_specs=[
            pl.BlockSpec((tm, d), lambda i, j: (i, 0)),
            pl.BlockSpec((1, d), const),
            pl.BlockSpec((d, tn), lambda i, j: (0, j)),
            pl.BlockSpec((d, LANE), const),
            pl.BlockSpec((LANE, GLA_QK), const),
            pl.BlockSpec((1, GLA_QK), const),
            pl.BlockSpec((LANE, GLA_QK), const),
            pl.BlockSpec((1, GLA_QK), const),
        ],
        out_specs=[
            pl.BlockSpec((tm, tn), lambda i, j: (i, j)),
            pl.BlockSpec((tm, GLA_QK), lambda i, j: (i, 0)),
            pl.BlockSpec((tm, GLA_QK), lambda i, j: (i, 0)),
        ],
        out_shape=[
            jax.ShapeDtypeStruct((t, n), BF16),
            jax.ShapeDtypeStruct((t, GLA_QK), F32),
            jax.ShapeDtypeStruct((t, GLA_QK), F32),
        ],
        scratch_shapes=[pltpu.VMEM((tm, d), BF16)],
        compiler_params=_params("parallel", "arbitrary"),
        name="mixin",
    )(x, g, w, wga, w2f, bf, w2b, bb)


def _split3(x):
    hi = x.astype(BF16)
    r1 = x - hi.astype(F32)
    mid = r1.astype(BF16)
    lo = (r1 - mid.astype(F32)).astype(BF16)
    return hi, mid, lo


def _gla_kernel(q_ref, k_ref, v_ref, r_ref, laf_ref, lab_ref, gn_ref, o_ref,
                oacc_ref, sf_ref, sb_ref, *, seq):
    c = GLA_CHUNK
    n_chunks = seq // c
    row = lax.broadcasted_iota(jnp.int32, (c, c), 0)
    col = lax.broadcasted_iota(jnp.int32, (c, c), 1)
    lower = col <= row
    upper = col >= row
    tri_f = lower.astype(BF16)
    tri_b = upper.astype(BF16)
    q_scale = GLA_DK ** -0.5

    sf_ref[...] = jnp.zeros_like(sf_ref)
    sb_ref[...] = jnp.zeros_like(sb_ref)

    def direction(rows, la_ref, tri, mask, last, st_ref):
        hi, mid, lo = _split3(la_ref[rows, :])
        b = _dot(tri, hi) + _dot(tri, mid) + _dot(tri, lo)
        bl = b[last:last + 1, :]
        q = q_ref[rows, :].astype(F32) * q_scale
        k = k_ref[rows, :].astype(F32)
        qt = (q * jnp.exp(b)).astype(BF16)
        kt = (k * jnp.exp(-b)).astype(BF16)
        ks = (k * jnp.exp(bl - b)).astype(BF16)
        v = v_ref[rows, :]
        a = jnp.where(mask, _dot_nt(qt, kt), 0.0).astype(BF16)
        st = st_ref[...]
        o = _dot(a, v) + _dot_nt(qt, st.astype(BF16))
        st_ref[...] = st * jnp.exp(bl) + _dot_tn(v, ks)
        return o

    def finish(rows, o):
        o = o * lax.rsqrt(jnp.mean(o * o, axis=-1, keepdims=True) + EPS) * gn_ref[...]
        r = r_ref[rows, :].astype(F32)
        o_ref[rows, :] = (o * (r * jax.nn.sigmoid(r))).astype(o_ref.dtype)

    def rows_of(n):
        return pl.ds(pl.multiple_of(n * c, c), c)

    def first_half(n, carry):
        rf, rb = rows_of(n), rows_of(n_chunks - 1 - n)
        oacc_ref[rf, :] = direction(rf, laf_ref, tri_f, lower, c - 1, sf_ref)
        oacc_ref[rb, :] = direction(rb, lab_ref, tri_b, upper, 0, sb_ref)
        return carry

    def second_half(n, carry):
        rf, rb = rows_of(n), rows_of(n_chunks - 1 - n)
        finish(rf, oacc_ref[rf, :] + direction(rf, laf_ref, tri_f, lower, c - 1, sf_ref))
        finish(rb, oacc_ref[rb, :] + direction(rb, lab_ref, tri_b, upper, 0, sb_ref))
        return carry

    lax.fori_loop(0, n_chunks // 2, first_half, 0)
    lax.fori_loop(n_chunks // 2, n_chunks, second_half, 0)


def _gla(z, laf, lab, gn):
    b, s, _ = z.shape
    assert s % (2 * GLA_CHUNK) == 0
    k_blk = GLA_QK // GLA_DK
    v_blk = 2 * GLA_QK // GLA_DV
    r_blk = (2 * GLA_QK + GLA_V) // GLA_DV
    return pl.pallas_call(
        functools.partial(_gla_kernel, seq=s),
        grid=(b, GLA_HEADS),
        in_specs=[
            pl.BlockSpec((None, s, GLA_DK), lambda i, h: (i, 0, h)),
            pl.BlockSpec((None, s, GLA_DK), lambda i, h: (i, 0, k_blk + h)),
            pl.BlockSpec((None, s, GLA_DV), lambda i, h: (i, 0, v_blk + h)),
            pl.BlockSpec((None, s, GLA_DV), lambda i, h: (i, 0, r_blk + h)),
            pl.BlockSpec((None, s, GLA_DK), lambda i, h: (i, 0, h)),
            pl.BlockSpec((None, s, GLA_DK), lambda i, h: (i, 0, h)),
            pl.BlockSpec((1, GLA_DV), lambda i, h: (0, h)),
        ],
        out_specs=pl.BlockSpec((None, s, GLA_DV), lambda i, h: (i, 0, h)),
        out_shape=jax.ShapeDtypeStruct((b, s, GLA_V), BF16),
        scratch_shapes=[
            pltpu.VMEM((s, GLA_DV), F32),
            pltpu.VMEM((GLA_DV, GLA_DK), F32),
            pltpu.VMEM((GLA_DV, GLA_DK), F32),
        ],
        compiler_params=_params("parallel", "parallel"),
        name="gla",
    )(z, z, z, z, laf, lab, gn)


def _rope(x, cos, sin):
    half = MLA_ROPE // 2
    return x * cos - pltpu.roll(x, LANE - half, 1) * sin + pltpu.roll(x, half, 1) * sin


def _mla_prep_kernel(cq_ref, ckv_ref, kpe_ref, gq_ref, wuq_ref, gkv_ref, wukv_ref,
                     gqq_ref, gqk_ref, cos_ref, sin_ref, q_ref, k_ref, v_ref):
    cos = cos_ref[...]
    sin = sin_ref[...]
    inv_w = 1.0 / MLA_QK_HEAD
    scale = MLA_QK_HEAD ** -0.5
    hp = MLA_HEAD_PAD

    qn = _rms(cq_ref[...].astype(F32), gq_ref[...]).astype(BF16)
    qf = _dot(qn, wuq_ref[...])
    gqq = gqq_ref[...]
    for h in range(MLA_HEADS):
        qh = qf[:, h * hp:(h + 1) * hp]
        r = lax.rsqrt(jnp.sum(qh * qh, axis=-1, keepdims=True) * inv_w + EPS)
        qh = qh * r * gqq
        q_ref[h, :, :MLA_NOPE] = (qh[:, :MLA_NOPE] * scale).astype(BF16)
        q_ref[h, :, MLA_NOPE:] = (_rope(qh[:, MLA_NOPE:], cos, sin) * scale).astype(BF16)

    kvn = _rms(ckv_ref[...].astype(F32), gkv_ref[...]).astype(BF16)
    kvf = _dot(kvn, wukv_ref[...])
    kpe = kpe_ref[...].astype(F32)
    kpe_ss = jnp.sum(kpe * kpe, axis=-1, keepdims=True)
    gqk = gqk_ref[...]
    hw = MLA_NOPE + MLA_DV
    for h in range(MLA_HEADS):
        kn = kvf[:, h * hw:h * hw + MLA_NOPE]
        r = lax.rsqrt((jnp.sum(kn * kn, axis=-1, keepdims=True) + kpe_ss) * inv_w + EPS)
        k_ref[h, :, :MLA_NOPE] = (kn * r * gqk[:, :MLA_NOPE]).astype(BF16)
        k_ref[h, :, MLA_NOPE:] = _rope(kpe * r * gqk[:, MLA_NOPE:], cos, sin).astype(BF16)
        v_ref[h] = kvf[:, h * hw + MLA_NOPE:(h + 1) * hw].astype(BF16)


def _mla_prep(z, gq, wuq, gkv, wukv, gqq, gqk, cos, sin, tm):
    b, s, _ = z.shape
    hp = MLA_HEAD_PAD
    const = lambda i, j: (0, 0)
    return pl.pallas_call(
        _mla_prep_kernel,
        grid=(b, s // tm),
        in_specs=[
            pl.BlockSpec((None, tm, MLA_Q_RANK), lambda i, j: (i, j, Z_CQ // MLA_Q_RANK)),
            pl.BlockSpec((None, tm, MLA_KV_RANK), lambda i, j: (i, j, Z_CKV // MLA_KV_RANK)),
            pl.BlockSpec((None, tm, LANE), lambda i, j: (i, j, Z_KPE // LANE)),
            pl.BlockSpec((1, MLA_Q_RANK), const),
            pl.BlockSpec((MLA_Q_RANK, MLA_HEADS * hp), const),
            pl.BlockSpec((1, MLA_KV_RANK), const),
            pl.BlockSpec((MLA_KV_RANK, MLA_HEADS * (MLA_NOPE + MLA_DV)), const),
            pl.BlockSpec((1, hp), const),
            pl.BlockSpec((1, hp), const),
            pl.BlockSpec((tm, LANE), lambda i, j: (j, 0)),
            pl.BlockSpec((tm, LANE), lambda i, j: (j, 0)),
        ],
        out_specs=[
            pl.BlockSpec((None, MLA_HEADS, tm, hp), lambda i, j: (i, 0, j, 0)),
            pl.BlockSpec((None, MLA_HEADS, tm, hp), lambda i, j: (i, 0, j, 0)),
            pl.BlockSpec((None, MLA_HEADS, tm, MLA_DV), lambda i, j: (i, 0, j, 0)),
        ],
        out_shape=[
            jax.ShapeDtypeStruct((b, MLA_HEADS, s, hp), BF16),
            jax.ShapeDtypeStruct((b, MLA_HEADS, s, hp), BF16),
            jax.ShapeDtypeStruct((b, MLA_HEADS, s, MLA_DV), BF16),
        ],
        compiler_params=_params("parallel", "parallel"),
        name="mla_prep",
    )(z, z, z, gq, wuq, gkv, wukv, gqq, gqk, cos, sin)


def _flash_kernel(q_ref, k_ref, v_ref, o_ref, *, seq, tk):
    q = q_ref[...]
    tq = q.shape[0]

    def body(n, carry):
        m, l, acc = carry
        rows = pl.ds(pl.multiple_of(n * tk, tk), tk)
        s = _dot_nt(q, k_ref[rows, :])
        m_new = jnp.maximum(m, jnp.max(s, axis=-1, keepdims=True))
        alpha = jnp.exp(m - m_new)
        p = jnp.exp(s - m_new)
        l = alpha * l + jnp.sum(p, axis=-1, keepdims=True)
        acc = alpha * acc + _dot(p.astype(BF16), v_ref[rows, :])
        return m_new, l, acc

    init = (jnp.full((tq, 1), -jnp.inf, F32), jnp.zeros((tq, 1), F32), jnp.zeros((tq, MLA_DV), F32))
    _, l, acc = lax.fori_loop(0, seq // tk, body, init)
    o_ref[...] = (acc / l).astype(o_ref.dtype)


def _flash(q, k, v, tq, tk):
    b, h, s, hp = q.shape
    return pl.pallas_call(
        functools.partial(_flash_kernel, seq=s, tk=tk),
        grid=(b, h, s // tq),
        in_specs=[
            pl.BlockSpec((None, None, tq, hp), lambda i, j, n: (i, j, n, 0)),
            pl.BlockSpec((None, None, s, hp), lambda i, j, n: (i, j, 0, 0)),
            pl.BlockSpec((None, None, s, MLA_DV), lambda i, j, n: (i, j, 0, 0)),
        ],
        out_specs=pl.BlockSpec((None, tq, MLA_DV), lambda i, j, n: (i, n, j)),
        out_shape=jax.ShapeDtypeStruct((b, s, h * MLA_DV), BF16),
        compiler_params=_params("parallel", "parallel", "arbitrary"),
        name="mla_flash",
    )(q, k, v)


def _outproj_kernel(a_ref, b_ref, wa_ref, wb_ref, res_ref, o_ref):
    o_ref[...] = res_ref[...] + _dot(a_ref[...], wa_ref[...]) + _dot(b_ref[...], wb_ref[...])


def _outproj(a, b, wa, wb, res, tm, tn):
    t, ka = a.shape
    kb = b.shape[1]
    n = wa.shape[1]
    return pl.pallas_call(
        _outproj_kernel,
        grid=(t // tm, n // tn),
        in_specs=[
            pl.BlockSpec((tm, ka), lambda i, j: (i, 0)),
            pl.BlockSpec((tm, kb), lambda i, j: (i, 0)),
            pl.BlockSpec((ka, tn), lambda i, j: (0, j)),
            pl.BlockSpec((kb, tn), lambda i, j: (0, j)),
            pl.BlockSpec((tm, tn), lambda i, j: (i, j)),
        ],
        out_specs=pl.BlockSpec((tm, tn), lambda i, j: (i, j)),
        out_shape=jax.ShapeDtypeStruct((t, n), F32),
        compiler_params=_params("parallel", "arbitrary"),
        name="outproj",
    )(a, b, wa, wb, res)


def _nmm_kernel(*refs, in_norm, head_dim, out_scale, has_res):
    it = iter(refs)
    x_ref = next(it)
    g_ref = next(it) if in_norm else None
    w_ref = next(it)
    hg_ref = next(it) if head_dim else None
    res_ref = next(it) if has_res else None
    o_ref = next(it)
    if in_norm:
        xn_ref = next(it)

        @pl.when(pl.program_id(1) == 0)
        def _():
            xn_ref[...] = _rms(x_ref[...], g_ref[...]).astype(BF16)

        xn = xn_ref[...]
    else:
        xn = x_ref[...]
    acc = _dot(xn, w_ref[...])
    if has_res:
        acc = acc + res_ref[...]
    if head_dim:
        hg = hg_ref[...] * out_scale
        for h in range(acc.shape[1] // head_dim):
            sl = slice(h * head_dim, (h + 1) * head_dim)
            o_ref[:, sl] = _rms(acc[:, sl], hg).astype(o_ref.dtype)
    else:
        o_ref[...] = acc.astype(o_ref.dtype)


def _nmm(x, w, tm, tn, out_dtype, g=None, head_gain=None, out_scale=1.0, res=None, name="nmm"):
    t, k = x.shape
    n = w.shape[1]
    in_norm = g is not None
    head_dim = 0 if head_gain is None else head_gain.shape[1]
    const = lambda i, j: (0, 0)
    args, specs = [x], [pl.BlockSpec((tm, k), lambda i, j: (i, 0))]
    if in_norm:
        args.append(g)
        specs.append(pl.BlockSpec((1, k), const))
    args.append(w)
    specs.append(pl.BlockSpec((k, tn), lambda i, j: (0, j)))
    if head_dim:
        assert tn % head_dim == 0
        args.append(head_gain)
        specs.append(pl.BlockSpec((1, head_dim), const))
    if res is not None:
        args.append(res)
        specs.append(pl.BlockSpec((tm, tn), lambda i, j: (i, j)))
    return pl.pallas_call(
        functools.partial(_nmm_kernel, in_norm=in_norm, head_dim=head_dim, out_scale=out_scale,
                          has_res=res is not None),
        grid=(t // tm, n // tn),
        in_specs=specs,
        out_specs=pl.BlockSpec((tm, tn), lambda i, j: (i, j)),
        out_shape=jax.ShapeDtypeStruct((t, n), out_dtype),
        scratch_shapes=[pltpu.VMEM((tm, k), BF16)] if in_norm else [],
        compiler_params=_params("parallel", "arbitrary"),
        name=name,
    )(*args)


def _xattn_kernel(q_ref, k_ref, v_ref, o_ref):
    for h in range(MEM_HEADS):
        sl = slice(h * MEM_DH, (h + 1) * MEM_DH)
        s = _dot_nt(q_ref[:, sl], k_ref[:, sl])
        p = jnp.exp(s - jnp.max(s, axis=-1, keepdims=True))
        l = jnp.sum(p, axis=-1, keepdims=True)
        o_ref[:, sl] = (_dot(p.astype(BF16), v_ref[:, sl]) / l).astype(o_ref.dtype)


def _xattn(q, k, v, tm):
    b, s, d = q.shape
    m = k.shape[1]
    return pl.pallas_call(
        _xattn_kernel,
        grid=(b, s // tm),
        in_specs=[
            pl.BlockSpec((None, tm, d), lambda i, j: (i, j, 0)),
            pl.BlockSpec((None, m, d), lambda i, j: (i, 0, 0)),
            pl.BlockSpec((None, m, d), lambda i, j: (i, 0, 0)),
        ],
        out_specs=pl.BlockSpec((None, tm, d), lambda i, j: (i, j, 0)),
        out_shape=jax.ShapeDtypeStruct((b, s, d), BF16),
        compiler_params=_params("parallel", "arbitrary"),
        name="xattn",
    )(q, k, v)


def _rope_tables(seq):
    half = MLA_ROPE // 2
    inv = ROPE_THETA ** (-jnp.arange(half, dtype=F32) / half)
    ang = jnp.arange(seq, dtype=jnp.int32).astype(F32)[:, None] * inv[None, :]
    pad = jnp.zeros((seq, LANE - MLA_ROPE), F32)
    cos = jnp.concatenate([jnp.cos(ang), jnp.cos(ang), pad], axis=-1)
    sin = jnp.concatenate([jnp.sin(ang), jnp.sin(ang), pad], axis=-1)
    return cos, sin


def _pad_cols(w, n):
    return jnp.pad(w, ((0, 0), (0, n - w.shape[1])))


def _layer(x, mem, P, tm=512):
    b, s, d = x.shape
    m = mem.shape[1]
    t = b * s
    row = lambda a: a.reshape(1, -1).astype(F32)

    def ffn_weights(pre):
        wg = _pad_cols(P[pre + "_w_gate"], D_FF_PAD).astype(BF16)
        wu = _pad_cols(P[pre + "_w_up"], D_FF_PAD).astype(BF16)
        wd = jnp.pad(P[pre + "_w_down"], ((0, D_FF_PAD - D_FF), (0, 0))).astype(BF16)
        return wg, wu, wd

    h = _ffn(x.reshape(t, d), row(P["ffn1_norm"]), *ffn_weights("ffn1"), tm)

    w_in = P["w_in"]
    g0, g1 = Z_GATE_START, Z_GATE_START + 2 * GLA_GATE_RANK
    w_main = _pad_cols(jnp.concatenate([w_in[:, :g0], w_in[:, g1:]], axis=1), Z_PAD).astype(BF16)
    w_ga = _pad_cols(w_in[:, g0:g1], LANE).astype(BF16)
    rank = GLA_GATE_RANK
    w2f = jnp.pad(P["gla_wa2_fwd"], ((0, LANE - rank), (0, 0))).astype(BF16)
    w2b = jnp.pad(P["gla_wa2_bwd"], ((rank, LANE - 2 * rank), (0, 0))).astype(BF16)
    z, laf, lab = _mixin(h, row(P["mix_norm"]), w_main, w_ga, w2f, row(P["gla_ba_fwd"]),
                         w2b, row(P["gla_ba_bwd"]), tm)
    z3 = z.reshape(b, s, Z_PAD)

    y_gla = _gla(z3, laf.reshape(b, s, GLA_QK), lab.reshape(b, s, GLA_QK), row(P["gla_out_norm"]))

    hp = MLA_HEAD_PAD
    wuq = P["mla_w_uq"].reshape(MLA_Q_RANK, MLA_HEADS, MLA_QK_HEAD)
    wuq = jnp.pad(wuq, ((0, 0), (0, 0), (0, hp - MLA_QK_HEAD))).reshape(MLA_Q_RANK, MLA_HEADS * hp).astype(BF16)
    gqq = _pad_cols(row(P["mla_qk_q_norm"]), hp)
    gqk = _pad_cols(row(P["mla_qk_k_norm"]), hp)
    cos, sin = _rope_tables(s)
    q, k, v = _mla_prep(z3, row(P["mla_q_norm"]), wuq, row(P["mla_kv_norm"]), P["mla_w_ukv"].astype(BF16),
                        gqq, gqk, cos, sin, tm)
    y_mla = _flash(q, k, v, tq=256, tk=512)

    w_out = P["w_out"].astype(BF16)
    h = _outproj(y_gla.reshape(t, GLA_V), y_mla.reshape(t, MLA_V), w_out[:GLA_V], w_out[GLA_V:], h, tm, 1024)

    mem2 = mem.reshape(b * m, d)
    tmm = 256
    xk = _nmm(mem2, P["xattn_wk"].astype(BF16), tmm, 1024, BF16, g=row(P["mem_norm"]),
              head_gain=row(P["xattn_k_norm"]), name="mem_k")
    xv = _nmm(mem2, P["xattn_wv"].astype(BF16), tmm, 1024, BF16, g=row(P["mem_norm"]), name="mem_v")
    xq = _nmm(h, P["xattn_wq"].astype(BF16), tm, 1024, BF16, g=row(P["xattn_norm"]),
              head_gain=row(P["xattn_q_norm"]), out_scale=MEM_DH ** -0.5, name="xattn_q")
    xo = _xattn(xq.reshape(b, s, d), xk.reshape(b, m, d), xv.reshape(b, m, d), tm)
    h = _nmm(xo.reshape(t, d), P["xattn_wo"].astype(BF16), tm, 1024, F32, res=h, name="xattn_o")

    h = _ffn(h, row(P["ffn2_norm"]), *ffn_weights("ffn2"), tm)
    return h.reshape(b, s, d)


def kernel(x_prompt, x_sample, mem_prompt, mem_sample, ffn1_norm, ffn1_w_gate, ffn1_w_up, ffn1_w_down, mix_norm, w_in, gla_wa2_fwd, gla_ba_fwd, gla_wa2_bwd, gla_ba_bwd, gla_out_norm, mla_q_norm, mla_w_uq, mla_kv_norm, mla_w_ukv, mla_qk_q_norm, mla_qk_k_norm, w_out, xattn_norm, mem_norm, xattn_wq, xattn_wk, xattn_wv, xattn_q_norm, xattn_k_norm, xattn_wo, ffn2_norm, ffn2_w_gate, ffn2_w_up, ffn2_w_down):
    stacked = dict(
        ffn1_norm=ffn1_norm, ffn1_w_gate=ffn1_w_gate, ffn1_w_up=ffn1_w_up, ffn1_w_down=ffn1_w_down,
        mix_norm=mix_norm, w_in=w_in,
        gla_wa2_fwd=gla_wa2_fwd, gla_ba_fwd=gla_ba_fwd, gla_wa2_bwd=gla_wa2_bwd, gla_ba_bwd=gla_ba_bwd,
        gla_out_norm=gla_out_norm,
        mla_q_norm=mla_q_norm, mla_w_uq=mla_w_uq, mla_kv_norm=mla_kv_norm, mla_w_ukv=mla_w_ukv,
        mla_qk_q_norm=mla_qk_q_norm, mla_qk_k_norm=mla_qk_k_norm,
        w_out=w_out,
        xattn_norm=xattn_norm, mem_norm=mem_norm, xattn_wq=xattn_wq, xattn_wk=xattn_wk, xattn_wv=xattn_wv,
        xattn_q_norm=xattn_q_norm, xattn_k_norm=xattn_k_norm, xattn_wo=xattn_wo,
        ffn2_norm=ffn2_norm, ffn2_w_gate=ffn2_w_gate, ffn2_w_up=ffn2_w_up, ffn2_w_down=ffn2_w_down,
    )
    n_prompt = x_prompt.shape[0]
    h = jnp.concatenate([x_prompt, x_sample], axis=0)
    mem = jnp.concatenate([mem_prompt, mem_sample], axis=0)
    for l in range(ffn1_norm.shape[0]):
        h = _layer(h, mem, {name: a[l] for name, a in stacked.items()})
    return (h[:n_prompt], h[n_prompt:])
```

```python
import functools

import jax
import jax.numpy as jnp
from jax import lax
from jax.experimental import pallas as pl
from jax.experimental.pallas import tpu as pltpu

F32 = jnp.float32
BF16 = jnp.bfloat16

D_MODEL = 2048
D_FF = 5504
MEM_HEADS = 4
MEM_DH = D_MODEL // MEM_HEADS
GLA_HEADS = 4
GLA_DK = 128
GLA_DV = 256
GLA_GATE_RANK = 16
GLA_GATE_TEMP = 16.0
GLA_CHUNK = 64
MLA_HEADS = 8
MLA_Q_RANK = 512
MLA_KV_RANK = 256
MLA_NOPE = 128
MLA_ROPE = 64
MLA_DV = 128
ROPE_THETA = 10000.0
EPS = 1e-6
LOG2_E = 1.4426950408889634

GLA_QK = GLA_HEADS * GLA_DK
GLA_V = GLA_HEADS * GLA_DV
MLA_V = MLA_HEADS * MLA_DV
MLA_QK_HEAD = MLA_NOPE + MLA_ROPE

LANE = 128
MXU_DIM = 256
VMEM_LIMIT = 60 * 1024 * 1024

MLA_HEAD_PAD = MXU_DIM
GLA_BLOCK = MXU_DIM
D_FF_TILE = 512
D_FF_PAD = -(-D_FF // D_FF_TILE) * D_FF_TILE
FFN_ROWS = 1024
PROJ_ROWS = 512
MEM_ROWS = 256
PROJ_COLS = 1024
FLASH_Q = 1024
FLASH_K = 512
Z_GATE_START = 2 * GLA_QK + 2 * GLA_V
Z_MAIN = Z_GATE_START + MLA_Q_RANK + MLA_KV_RANK + MLA_ROPE
Z_PAD = -(-Z_MAIN // PROJ_COLS) * PROJ_COLS
Z_CQ = Z_GATE_START
Z_CKV = Z_CQ + MLA_Q_RANK
Z_KPE = Z_CKV + MLA_KV_RANK


def _params(*sem):
    return pltpu.CompilerParams(dimension_semantics=sem, vmem_limit_bytes=VMEM_LIMIT)


def _resident(shape):
    return pl.BlockSpec(shape, lambda *_: (0,) * len(shape), pipeline_mode=pl.Buffered(1))


def _rms(x, g):
    return x * lax.rsqrt(jnp.mean(x * x, axis=-1, keepdims=True) + EPS) * g


def _dot(a, b):
    return jnp.dot(a, b, preferred_element_type=F32)


def _dot_nt(a, b):
    return lax.dot_general(a, b, (((1,), (1,)), ((), ())), preferred_element_type=F32)


def _dot_tn(a, b):
    return lax.dot_general(a, b, (((0,), (0,)), ((), ())), preferred_element_type=F32)


def _ffn_kernel(x_ref, g_ref, wg_ref, wu_ref, wd_ref, o_ref, xn_ref):
    j = pl.program_id(1)

    @pl.when(j == 0)
    def _():
        xn_ref[...] = _rms(x_ref[...], g_ref[...]).astype(BF16)
        o_ref[...] = jnp.zeros_like(o_ref)

    xn = xn_ref[...]
    gate = _dot(xn, wg_ref[...])
    up = _dot(xn, wu_ref[...])
    hid = (gate * jax.nn.sigmoid(gate) * up).astype(BF16)
    o_ref[...] += _dot(hid, wd_ref[...])

    @pl.when(j == pl.num_programs(1) - 1)
    def _():
        o_ref[...] = x_ref[...] + 0.5 * o_ref[...]


def _ffn(x, g, wg, wu, wd):
    t, d = x.shape
    f = wg.shape[1]
    tm = min(FFN_ROWS, t)
    tf = D_FF_TILE
    return pl.pallas_call(
        _ffn_kernel,
        grid=(t // tm, f // tf),
        in_specs=[
            pl.BlockSpec((tm, d), lambda i, j: (i, 0)),
            pl.BlockSpec((1, d), lambda i, j: (0, 0)),
            pl.BlockSpec((d, tf), lambda i, j: (0, j)),
            pl.BlockSpec((d, tf), lambda i, j: (0, j)),
            pl.BlockSpec((tf, d), lambda i, j: (j, 0)),
        ],
        out_specs=pl.BlockSpec((tm, d), lambda i, j: (i, 0)),
        out_shape=jax.ShapeDtypeStruct((t, d), F32),
        scratch_shapes=[pltpu.VMEM((tm, d), BF16)],
        compiler_params=_params("parallel", "arbitrary"),
        name="ffn",
    )(x, g, wg, wu, wd)


def _log_sigmoid(x):
    return jnp.minimum(x, 0.0) - jnp.log(1.0 + jnp.exp(-jnp.abs(x)))


def _mixin_kernel(x_ref, g_ref, w_ref, wga_ref, w2f_ref, bf_ref, w2b_ref, bb_ref,
                  z_ref, laf_ref, lab_ref):
    xn = _rms(x_ref[...], g_ref[...]).astype(BF16)
    c = _dot(xn, wga_ref[...]).astype(BF16)
    laf_ref[...] = _log_sigmoid(_dot(c, w2f_ref[...]) + bf_ref[...]) * (1.0 / GLA_GATE_TEMP)
    lab_ref[...] = _log_sigmoid(_dot(c, w2b_ref[...]) + bb_ref[...]) * (1.0 / GLA_GATE_TEMP)
    for n in range(z_ref.shape[1] // PROJ_COLS):
        sl = slice(n * PROJ_COLS, (n + 1) * PROJ_COLS)
        z_ref[:, sl] = _dot(xn, w_ref[:, sl]).astype(z_ref.dtype)


def _mixin(x, g, w, wga, w2f, bf, w2b, bb):
    t, d = x.shape
    n = w.shape[1]
    tm = min(PROJ_ROWS, t)
    rows = lambda width: pl.BlockSpec((tm, width), lambda i: (i, 0))
    return pl.pallas_call(
        _mixin_kernel,
        grid=(t // tm,),
        in_specs=[rows(d), _resident((1, d)), _resident((d, n)), _resident((d, LANE)),
                  _resident((LANE, GLA_QK)), _resident((1, GLA_QK)),
                  _resident((LANE, GLA_QK)), _resident((1, GLA_QK))],
        out_specs=[rows(n), rows(GLA_QK), rows(GLA_QK)],
        out_shape=[
            jax.ShapeDtypeStruct((t, n), BF16),
            jax.ShapeDtypeStruct((t, GLA_QK), F32),
            jax.ShapeDtypeStruct((t, GLA_QK), F32),
        ],
        compiler_params=_params("parallel"),
        name="mixin",
    )(x, g, w, wga, w2f, bf, w2b, bb)


def _split3(x):
    hi = x.astype(BF16)
    r1 = x - hi.astype(F32)
    mid = r1.astype(BF16)
    lo = (r1 - mid.astype(F32)).astype(BF16)
    return hi, mid, lo


def _gla_kernel(q_ref, k_ref, v_ref, r_ref, laf_ref, lab_ref, gn_ref, o_ref,
                oacc_ref, sf_ref, sb_ref, *, seq):
    c = GLA_CHUNK
    blk = GLA_BLOCK
    cpb = blk // c
    n_blocks = seq // blk
    row = lax.broadcasted_iota(jnp.int32, (blk, blk), 0)
    col = lax.broadcasted_iota(jnp.int32, (blk, blk), 1)
    same_chunk = (row // c) == (col // c)
    lower = same_chunk & (col <= row)
    upper = same_chunk & (col >= row)
    tri_f = lower.astype(BF16)
    tri_b = upper.astype(BF16)
    q_scale = GLA_DK ** -0.5

    sf_ref[...] = jnp.zeros_like(sf_ref)
    sb_ref[...] = jnp.zeros_like(sb_ref)

    def direction(rows, la_ref, tri, mask, forward, st_ref):
        hi, mid, lo = _split3(la_ref[rows, :])
        b = _dot(tri, hi) + _dot(tri, mid) + _dot(tri, lo)
        last = c - 1 if forward else 0
        b_end = [b[i * c + last:i * c + last + 1, :] for i in range(cpb)]
        bl = jnp.concatenate([jnp.broadcast_to(r, (c, GLA_DK)) for r in b_end], axis=0)
        q = q_ref[rows, :].astype(F32) * q_scale
        k = k_ref[rows, :].astype(F32)
        qt = (q * jnp.exp(b)).astype(BF16)
        kt = (k * jnp.exp(-b)).astype(BF16)
        ks = (k * jnp.exp(bl - b)).astype(BF16)
        v = v_ref[rows, :]
        a = jnp.where(mask, _dot_nt(qt, kt), 0.0).astype(BF16)
        o_intra = _dot(a, v)
        st = st_ref[...]
        outs = [None] * cpb
        for i in (range(cpb) if forward else reversed(range(cpb))):
            sl = slice(i * c, (i + 1) * c)
            outs[i] = o_intra[sl] + _dot_nt(qt[sl], st.astype(BF16))
            st = st * jnp.exp(b_end[i]) + _dot_tn(v[sl], ks[sl])
        st_ref[...] = st
        return jnp.concatenate(outs, axis=0)

    def finish(rows, o):
        o = o * lax.rsqrt(jnp.mean(o * o, axis=-1, keepdims=True) + EPS) * gn_ref[...]
        r = r_ref[rows, :].astype(F32)
        o_ref[rows, :] = (o * (r * jax.nn.sigmoid(r))).astype(o_ref.dtype)

    def rows_of(n):
        return pl.ds(pl.multiple_of(n * blk, blk), blk)

    def first_half(n, carry):
        rf, rb = rows_of(n), rows_of(n_blocks - 1 - n)
        oacc_ref[rf, :] = direction(rf, laf_ref, tri_f, lower, True, sf_ref)
        oacc_ref[rb, :] = direction(rb, lab_ref, tri_b, upper, False, sb_ref)
        return carry

    def second_half(n, carry):
        rf, rb = rows_of(n), rows_of(n_blocks - 1 - n)
        finish(rf, oacc_ref[rf, :] + direction(rf, laf_ref, tri_f, lower, True, sf_ref))
        finish(rb, oacc_ref[rb, :] + direction(rb, lab_ref, tri_b, upper, False, sb_ref))
        return carry

    lax.fori_loop(0, n_blocks // 2, first_half, 0)
    lax.fori_loop(n_blocks // 2, n_blocks, second_half, 0)


def _gla(z, laf, lab, gn):
    b, s, _ = z.shape
    assert s % (2 * GLA_BLOCK) == 0
    k_blk = GLA_QK // GLA_DK
    v_blk = 2 * GLA_QK // GLA_DV
    r_blk = (2 * GLA_QK + GLA_V) // GLA_DV
    return pl.pallas_call(
        functools.partial(_gla_kernel, seq=s),
        grid=(b, GLA_HEADS),
        in_specs=[
            pl.BlockSpec((None, s, GLA_DK), lambda i, h: (i, 0, h)),
            pl.BlockSpec((None, s, GLA_DK), lambda i, h: (i, 0, k_blk + h)),
            pl.BlockSpec((None, s, GLA_DV), lambda i, h: (i, 0, v_blk + h)),
            pl.BlockSpec((None, s, GLA_DV), lambda i, h: (i, 0, r_blk + h)),
            pl.BlockSpec((None, s, GLA_DK), lambda i, h: (i, 0, h)),
            pl.BlockSpec((None, s, GLA_DK), lambda i, h: (i, 0, h)),
            pl.BlockSpec((1, GLA_DV), lambda i, h: (0, h)),
        ],
        out_specs=pl.BlockSpec((None, s, GLA_DV), lambda i, h: (i, 0, h)),
        out_shape=jax.ShapeDtypeStruct((b, s, GLA_V), BF16),
        scratch_shapes=[
            pltpu.VMEM((s, GLA_DV), F32),
            pltpu.VMEM((GLA_DV, GLA_DK), F32),
            pltpu.VMEM((GLA_DV, GLA_DK), F32),
        ],
        compiler_params=_params("parallel", "parallel"),
        name="gla",
    )(z, z, z, z, laf, lab, gn)


def _rope(x, cos, sin):
    half = MLA_ROPE // 2
    return x * cos - pltpu.roll(x, LANE - half, 1) * sin + pltpu.roll(x, half, 1) * sin


def _mla_prep_kernel(cq_ref, ckv_ref, kpe_ref, gq_ref, wuq_ref, gkv_ref, wukv_ref,
                     gqq_ref, gqk_ref, cos_ref, sin_ref, q_ref, k_ref, v_ref):
    cos = cos_ref[...]
    sin = sin_ref[...]
    inv_w = 1.0 / MLA_QK_HEAD
    scale = MLA_QK_HEAD ** -0.5 * LOG2_E
    hp = MLA_HEAD_PAD

    qn = _rms(cq_ref[...].astype(F32), gq_ref[...]).astype(BF16)
    qf = _dot(qn, wuq_ref[...])
    gqq = gqq_ref[...]
    for h in range(MLA_HEADS):
        qh = qf[:, h * hp:(h + 1) * hp]
        r = lax.rsqrt(jnp.sum(qh * qh, axis=-1, keepdims=True) * inv_w + EPS)
        qh = qh * r * gqq
        q_ref[h, :, :MLA_NOPE] = (qh[:, :MLA_NOPE] * scale).astype(BF16)
        q_ref[h, :, MLA_NOPE:] = (_rope(qh[:, MLA_NOPE:], cos, sin) * scale).astype(BF16)

    kvn = _rms(ckv_ref[...].astype(F32), gkv_ref[...]).astype(BF16)
    kvf = _dot(kvn, wukv_ref[...])
    kpe = kpe_ref[...].astype(F32)
    kpe_ss = jnp.sum(kpe * kpe, axis=-1, keepdims=True)
    gqk = gqk_ref[...]
    hw = MLA_NOPE + MLA_DV
    for h in range(MLA_HEADS):
        kn = kvf[:, h * hw:h * hw + MLA_NOPE]
        r = lax.rsqrt((jnp.sum(kn * kn, axis=-1, keepdims=True) + kpe_ss) * inv_w + EPS)
        k_ref[h, :, :MLA_NOPE] = (kn * r * gqk[:, :MLA_NOPE]).astype(BF16)
        k_ref[h, :, MLA_NOPE:] = _rope(kpe * r * gqk[:, MLA_NOPE:], cos, sin).astype(BF16)
        v_ref[h] = kvf[:, h * hw + MLA_NOPE:(h + 1) * hw].astype(BF16)


def _mla_prep(z, gq, wuq, gkv, wukv, gqq, gqk, cos, sin):
    b, s, _ = z.shape
    hp = MLA_HEAD_PAD
    tm = min(PROJ_ROWS, s)
    return pl.pallas_call(
        _mla_prep_kernel,
        grid=(b, s // tm),
        in_specs=[
            pl.BlockSpec((None, tm, MLA_Q_RANK), lambda i, j: (i, j, Z_CQ // MLA_Q_RANK)),
            pl.BlockSpec((None, tm, MLA_KV_RANK), lambda i, j: (i, j, Z_CKV // MLA_KV_RANK)),
            pl.BlockSpec((None, tm, LANE), lambda i, j: (i, j, Z_KPE // LANE)),
            _resident((1, MLA_Q_RANK)),
            _resident((MLA_Q_RANK, MLA_HEADS * hp)),
            _resident((1, MLA_KV_RANK)),
            _resident((MLA_KV_RANK, MLA_HEADS * (MLA_NOPE + MLA_DV))),
            _resident((1, hp)),
            _resident((1, hp)),
            pl.BlockSpec((tm, LANE), lambda i, j: (j, 0)),
            pl.BlockSpec((tm, LANE), lambda i, j: (j, 0)),
        ],
        out_specs=[
            pl.BlockSpec((None, MLA_HEADS, tm, hp), lambda i, j: (i, 0, j, 0)),
            pl.BlockSpec((None, MLA_HEADS, tm, hp), lambda i, j: (i, 0, j, 0)),
            pl.BlockSpec((None, MLA_HEADS, tm, MLA_DV), lambda i, j: (i, 0, j, 0)),
        ],
        out_shape=[
            jax.ShapeDtypeStruct((b, MLA_HEADS, s, hp), BF16),
            jax.ShapeDtypeStruct((b, MLA_HEADS, s, hp), BF16),
            jax.ShapeDtypeStruct((b, MLA_HEADS, s, MLA_DV), BF16),
        ],
        compiler_params=_params("parallel", "parallel"),
        name="mla_prep",
    )(z, z, z, gq, wuq, gkv, wukv, gqq, gqk, cos, sin)


def _flash_kernel(q_ref, k_ref, v_ref, o_ref, *, seq, tk):
    q = q_ref[...]
    m = l = acc = None
    for n in range(seq // tk):
        rows = pl.ds(n * tk, tk)
        s = _dot_nt(q, k_ref[rows, :])
        s_max = jnp.max(s, axis=-1, keepdims=True)
        if n == 0:
            m = s_max
            p = jnp.exp2(s - m)
            l = jnp.sum(p, axis=-1, keepdims=True)
            acc = _dot(p.astype(BF16), v_ref[rows, :])
        else:
            m_new = jnp.maximum(m, s_max)
            alpha = jnp.exp2(m - m_new)
            p = jnp.exp2(s - m_new)
            l = alpha * l + jnp.sum(p, axis=-1, keepdims=True)
            acc = alpha * acc + _dot(p.astype(BF16), v_ref[rows, :])
            m = m_new
    o_ref[...] = (acc / l).astype(o_ref.dtype)


def _flash(q, k, v):
    b, h, s, hp = q.shape
    tq = min(FLASH_Q, s)
    tk = min(FLASH_K, s)
    return pl.pallas_call(
        functools.partial(_flash_kernel, seq=s, tk=tk),
        grid=(b, h, s // tq),
        in_specs=[
            pl.BlockSpec((None, None, tq, hp), lambda i, j, n: (i, j, n, 0)),
            pl.BlockSpec((None, None, s, hp), lambda i, j, n: (i, j, 0, 0)),
            pl.BlockSpec((None, None, s, MLA_DV), lambda i, j, n: (i, j, 0, 0)),
        ],
        out_specs=pl.BlockSpec((None, tq, MLA_DV), lambda i, j, n: (i, n, j)),
        out_shape=jax.ShapeDtypeStruct((b, s, h * MLA_DV), BF16),
        compiler_params=_params("parallel", "parallel", "arbitrary"),
        name="mla_flash",
    )(q, k, v)


def _post1_kernel(a_ref, b_ref, res_ref, wa_ref, wb_ref, g_ref, wq_ref, hg_ref, h_ref, q_ref):
    h = res_ref[...] + _dot(a_ref[...], wa_ref[...]) + _dot(b_ref[...], wb_ref[...])
    h_ref[...] = h
    xn = _rms(h, g_ref[...]).astype(BF16)
    hg = hg_ref[...] * (MEM_DH ** -0.5 * LOG2_E)
    for n in range(q_ref.shape[1] // PROJ_COLS):
        sl = slice(n * PROJ_COLS, (n + 1) * PROJ_COLS)
        acc = _dot(xn, wq_ref[:, sl])
        for h2 in range(PROJ_COLS // MEM_DH):
            hs = slice(h2 * MEM_DH, (h2 + 1) * MEM_DH)
            q_ref[:, n * PROJ_COLS + h2 * MEM_DH:n * PROJ_COLS + (h2 + 1) * MEM_DH] = (
                _rms(acc[:, hs], hg).astype(q_ref.dtype))


def _post1(a, b, res, wa, wb, g, wq, hg):
    t, d = res.shape
    tm = min(PROJ_ROWS, t)
    rows = lambda width: pl.BlockSpec((tm, width), lambda i: (i, 0))
    return pl.pallas_call(
        _post1_kernel,
        grid=(t // tm,),
        in_specs=[rows(a.shape[1]), rows(b.shape[1]), rows(d),
                  _resident(wa.shape), _resident(wb.shape), _resident((1, d)),
                  _resident(wq.shape), _resident((1, MEM_DH))],
        out_specs=[rows(d), rows(d)],
        out_shape=[jax.ShapeDtypeStruct((t, d), F32), jax.ShapeDtypeStruct((t, d), BF16)],
        compiler_params=_params("parallel"),
        name="post1",
    )(a, b, res, wa, wb, g, wq, hg)


def _post2_kernel(q_ref, k_ref, v_ref, res_ref, wo_ref, o_ref):
    acc = res_ref[...]
    for h in range(MEM_HEADS):
        sl = slice(h * MEM_DH, (h + 1) * MEM_DH)
        s = _dot_nt(q_ref[:, sl], k_ref[:, sl])
        p = jnp.exp2(s - jnp.max(s, axis=-1, keepdims=True))
        l = jnp.sum(p, axis=-1, keepdims=True)
        oh = (_dot(p.astype(BF16), v_ref[:, sl]) / l).astype(BF16)
        acc = acc + _dot(oh, wo_ref[sl, :])
    o_ref[...] = acc


def _post2(q, k, v, res, wo):
    b, s, d = q.shape
    m = k.shape[1]
    tm = min(PROJ_ROWS, s)
    tok = pl.BlockSpec((None, tm, d), lambda i, j: (i, j, 0))
    mem = pl.BlockSpec((None, m, d), lambda i, j: (i, 0, 0))
    return pl.pallas_call(
        _post2_kernel,
        grid=(b, s // tm),
        in_specs=[tok, mem, mem, tok, _resident(wo.shape)],
        out_specs=tok,
        out_shape=jax.ShapeDtypeStruct((b, s, d), F32),
        compiler_params=_params("parallel", "arbitrary"),
        name="post2",
    )(q, k, v, res, wo)


def _memkv_kernel(x_ref, g_ref, wk_ref, wv_ref, hg_ref, k_ref, v_ref):
    xn = _rms(x_ref[...], g_ref[...]).astype(BF16)
    hg = hg_ref[...]
    for n in range(k_ref.shape[1] // PROJ_COLS):
        sl = slice(n * PROJ_COLS, (n + 1) * PROJ_COLS)
        acc = _dot(xn, wk_ref[:, sl])
        for h in range(PROJ_COLS // MEM_DH):
            hs = slice(h * MEM_DH, (h + 1) * MEM_DH)
            k_ref[:, n * PROJ_COLS + h * MEM_DH:n * PROJ_COLS + (h + 1) * MEM_DH] = (
                _rms(acc[:, hs], hg).astype(k_ref.dtype))
        v_ref[:, sl] = _dot(xn, wv_ref[:, sl]).astype(v_ref.dtype)


def _memkv(x, g, wk, wv, hg):
    t, d = x.shape
    tm = min(MEM_ROWS, t)
    rows = pl.BlockSpec((tm, d), lambda i: (i, 0))
    return pl.pallas_call(
        _memkv_kernel,
        grid=(t // tm,),
        in_specs=[rows, _resident((1, d)), _resident(wk.shape), _resident(wv.shape), _resident((1, MEM_DH))],
        out_specs=[rows, rows],
        out_shape=[jax.ShapeDtypeStruct((t, d), BF16), jax.ShapeDtypeStruct((t, d), BF16)],
        compiler_params=_params("parallel"),
        name="mem_kv",
    )(x, g, wk, wv, hg)


def _rope_tables(seq):
    half = MLA_ROPE // 2
    inv = ROPE_THETA ** (-jnp.arange(half, dtype=F32) / half)
    ang = jnp.arange(seq, dtype=jnp.int32).astype(F32)[:, None] * inv[None, :]
    pad = jnp.zeros((seq, LANE - MLA_ROPE), F32)
    cos = jnp.concatenate([jnp.cos(ang), jnp.cos(ang), pad], axis=-1)
    sin = jnp.concatenate([jnp.sin(ang), jnp.sin(ang), pad], axis=-1)
    return cos, sin


def _pad_cols(w, n):
    return jnp.pad(w, ((0, 0), (0, n - w.shape[1])))


def _prepare(P):
    row = lambda a: a.reshape(1, -1).astype(F32)
    W = {}
    for pre in ("ffn1", "ffn2"):
        W[pre] = (row(P[pre + "_norm"]),
                  _pad_cols(P[pre + "_w_gate"].astype(BF16), D_FF_PAD),
                  _pad_cols(P[pre + "_w_up"].astype(BF16), D_FF_PAD),
                  jnp.pad(P[pre + "_w_down"].astype(BF16), ((0, D_FF_PAD - D_FF), (0, 0))))
    w_in = P["w_in"].astype(BF16)
    g0, g1 = Z_GATE_START, Z_GATE_START + 2 * GLA_GATE_RANK
    rank = GLA_GATE_RANK
    W["mixin"] = (row(P["mix_norm"]),
                  _pad_cols(jnp.concatenate([w_in[:, :g0], w_in[:, g1:]], axis=1), Z_PAD),
                  _pad_cols(w_in[:, g0:g1], LANE),
                  jnp.pad(P["gla_wa2_fwd"].astype(BF16), ((0, LANE - rank), (0, 0))), row(P["gla_ba_fwd"]),
                  jnp.pad(P["gla_wa2_bwd"].astype(BF16), ((rank, LANE - 2 * rank), (0, 0))), row(P["gla_ba_bwd"]))
    W["gla_norm"] = row(P["gla_out_norm"])
    hp = MLA_HEAD_PAD
    wuq = P["mla_w_uq"].astype(BF16).reshape(MLA_Q_RANK, MLA_HEADS, MLA_QK_HEAD)
    wuq = jnp.pad(wuq, ((0, 0), (0, 0), (0, hp - MLA_QK_HEAD))).reshape(MLA_Q_RANK, MLA_HEADS * hp)
    W["mla"] = (row(P["mla_q_norm"]), wuq, row(P["mla_kv_norm"]), P["mla_w_ukv"].astype(BF16),
                _pad_cols(row(P["mla_qk_q_norm"]), hp), _pad_cols(row(P["mla_qk_k_norm"]), hp))
    w_out = P["w_out"].astype(BF16)
    W["post1"] = (w_out[:GLA_V], w_out[GLA_V:], row(P["xattn_norm"]), P["xattn_wq"].astype(BF16),
                  row(P["xattn_q_norm"]))
    W["memkv"] = (row(P["mem_norm"]), P["xattn_wk"].astype(BF16), P["xattn_wv"].astype(BF16),
                  row(P["xattn_k_norm"]))
    W["wo"] = P["xattn_wo"].astype(BF16)
    return W


def _layer(x, mem, W):
    b, s, d = x.shape
    m = mem.shape[1]
    t = b * s

    h = _ffn(x.reshape(t, d), *W["ffn1"])

    z, laf, lab = _mixin(h, *W["mixin"])
    z3 = z.reshape(b, s, Z_PAD)
    y_gla = _gla(z3, laf.reshape(b, s, GLA_QK), lab.reshape(b, s, GLA_QK), W["gla_norm"])
    q, k, v = _mla_prep(z3, *W["mla"], *_rope_tables(s))
    y_mla = _flash(q, k, v)

    h, xq = _post1(y_gla.reshape(t, GLA_V), y_mla.reshape(t, MLA_V), h, *W["post1"])
    xk, xv = _memkv(mem.reshape(b * m, d), *W["memkv"])
    h = _post2(xq.reshape(b, s, d), xk.reshape(b, m, d), xv.reshape(b, m, d), h.reshape(b, s, d), W["wo"])

    h = _ffn(h.reshape(t, d), *W["ffn2"])
    return h.reshape(b, s, d)


def kernel(x_prompt, x_sample, mem_prompt, mem_sample, ffn1_norm, ffn1_w_gate, ffn1_w_up, ffn1_w_down, mix_norm, w_in, gla_wa2_fwd, gla_ba_fwd, gla_wa2_bwd, gla_ba_bwd, gla_out_norm, mla_q_norm, mla_w_uq, mla_kv_norm, mla_w_ukv, mla_qk_q_norm, mla_qk_k_norm, w_out, xattn_norm, mem_norm, xattn_wq, xattn_wk, xattn_wv, xattn_q_norm, xattn_k_norm, xattn_wo, ffn2_norm, ffn2_w_gate, ffn2_w_up, ffn2_w_down):
    stacked = dict(
        ffn1_norm=ffn1_norm, ffn1_w_gate=ffn1_w_gate, ffn1_w_up=ffn1_w_up, ffn1_w_down=ffn1_w_down,
        mix_norm=mix_norm, w_in=w_in,
        gla_wa2_fwd=gla_wa2_fwd, gla_ba_fwd=gla_ba_fwd, gla_wa2_bwd=gla_wa2_bwd, gla_ba_bwd=gla_ba_bwd,
        gla_out_norm=gla_out_norm,
        mla_q_norm=mla_q_norm, mla_w_uq=mla_w_uq, mla_kv_norm=mla_kv_norm, mla_w_ukv=mla_w_ukv,
        mla_qk_q_norm=mla_qk_q_norm, mla_qk_k_norm=mla_qk_k_norm,
        w_out=w_out,
        xattn_norm=xattn_norm, mem_norm=mem_norm, xattn_wq=xattn_wq, xattn_wk=xattn_wk, xattn_wv=xattn_wv,
        xattn_q_norm=xattn_q_norm, xattn_k_norm=xattn_k_norm, xattn_wo=xattn_wo,
        ffn2_norm=ffn2_norm, ffn2_w_gate=ffn2_w_gate, ffn2_w_up=ffn2_w_up, ffn2_w_down=ffn2_w_down,
    )
    y_prompt, y_sample = x_prompt, x_sample
    for l in range(ffn1_norm.shape[0]):
        W = _prepare({name: a[l] for name, a in stacked.items()})
        y_prompt = _layer(y_prompt, mem_prompt, W)
        y_sample = _layer(y_sample, mem_sample, W)
    return (y_prompt, y_sample)
```

```python
import functools

import jax
import jax.numpy as jnp
from jax import lax
from jax.experimental import pallas as pl
from jax.experimental.pallas import tpu as pltpu

F32 = jnp.float32
BF16 = jnp.bfloat16

D_MODEL = 2048
D_FF = 5504
MEM_HEADS = 4
MEM_DH = D_MODEL // MEM_HEADS
GLA_HEADS = 4
GLA_DK = 128
GLA_DV = 256
GLA_GATE_RANK = 16
GLA_GATE_TEMP = 16.0
GLA_CHUNK = 64
MLA_HEADS = 8
MLA_Q_RANK = 512
MLA_KV_RANK = 256
MLA_NOPE = 128
MLA_ROPE = 64
MLA_DV = 128
ROPE_THETA = 10000.0
EPS = 1e-6
LOG2_E = 1.4426950408889634

GLA_QK = GLA_HEADS * GLA_DK
GLA_V = GLA_HEADS * GLA_DV
MLA_V = MLA_HEADS * MLA_DV
MLA_QK_HEAD = MLA_NOPE + MLA_ROPE

LANE = 128
MXU_DIM = 256
VMEM_LIMIT = 60 * 1024 * 1024

MLA_HEAD_PAD = MXU_DIM
GLA_BLOCK = MXU_DIM
D_FF_TILE = 512
FFN_ROWS = 1024
PROJ_ROWS = 512
MEM_ROWS = 256
PROJ_COLS = 1024
FLASH_Q = 1024
FLASH_K = 1024
Z_GATE_START = 2 * GLA_QK + 2 * GLA_V
Z_MAIN = Z_GATE_START + MLA_Q_RANK + MLA_KV_RANK + MLA_ROPE
Z_PAD = -(-(Z_MAIN + LANE) // PROJ_COLS) * PROJ_COLS
Z_CQ = Z_GATE_START
Z_CKV = Z_CQ + MLA_Q_RANK
Z_KPE = Z_CKV + MLA_KV_RANK


def _params(*sem):
    return pltpu.CompilerParams(dimension_semantics=sem, vmem_limit_bytes=VMEM_LIMIT)


def _resident(shape):
    return pl.BlockSpec(shape, lambda *_: (0,) * len(shape), pipeline_mode=pl.Buffered(1))


def _rms(x, g):
    return x * lax.rsqrt(jnp.mean(x * x, axis=-1, keepdims=True) + EPS) * g


def _dot(a, b):
    return jnp.dot(a, b, preferred_element_type=F32)


def _dot_nt(a, b):
    return lax.dot_general(a, b, (((1,), (1,)), ((), ())), preferred_element_type=F32)


def _dot_tn(a, b):
    return lax.dot_general(a, b, (((0,), (0,)), ((), ())), preferred_element_type=F32)


def _ffn_kernel(x_ref, g_ref, wg_ref, wu_ref, wd_ref, o_ref, xn_ref, *, overlap):
    j = pl.program_id(1)
    last = pl.num_programs(1) - 1

    @pl.when(j == 0)
    def _():
        xn_ref[...] = _rms(x_ref[...], g_ref[...]).astype(BF16)
        o_ref[...] = jnp.zeros_like(o_ref)

    xn = xn_ref[...]
    gate = _dot(xn, wg_ref[...])
    up = _dot(xn, wu_ref[...])
    hid = gate * jax.nn.sigmoid(gate) * up
    if overlap:
        col = lax.broadcasted_iota(jnp.int32, (1, hid.shape[1]), 1)
        hid = jnp.where((col >= overlap) | (j != last), hid, 0.0)
    o_ref[...] += _dot(hid.astype(BF16), wd_ref[...])

    @pl.when(j == last)
    def _():
        o_ref[...] = x_ref[...] + 0.5 * o_ref[...]


def _ffn(x, g, wg, wu, wd):
    t, d = x.shape
    f = wg.shape[1]
    tm = min(FFN_ROWS, t)
    tf = D_FF_TILE
    nj = -(-f // tf)
    assert f % LANE == 0 and tf % LANE == 0
    start = lambda j: jnp.minimum(j * (tf // LANE), (f - tf) // LANE) * LANE
    return pl.pallas_call(
        functools.partial(_ffn_kernel, overlap=nj * tf - f),
        grid=(t // tm, nj),
        in_specs=[
            pl.BlockSpec((tm, d), lambda i, j: (i, 0)),
            pl.BlockSpec((1, d), lambda i, j: (0, 0)),
            pl.BlockSpec((pl.Element(d), pl.Element(tf)), lambda i, j: (0, start(j))),
            pl.BlockSpec((pl.Element(d), pl.Element(tf)), lambda i, j: (0, start(j))),
            pl.BlockSpec((pl.Element(tf), pl.Element(d)), lambda i, j: (start(j), 0)),
        ],
        out_specs=pl.BlockSpec((tm, d), lambda i, j: (i, 0)),
        out_shape=jax.ShapeDtypeStruct((t, d), F32),
        scratch_shapes=[pltpu.VMEM((tm, d), BF16)],
        compiler_params=_params("parallel", "arbitrary"),
        name="ffn",
    )(x, g, wg, wu, wd)


def _log_sigmoid(x):
    return jnp.minimum(x, 0.0) - jnp.log(1.0 + jnp.exp(-jnp.abs(x)))


def _mixin_kernel(x_ref, g_ref, w_ref, w2f_ref, bf_ref, w2b_ref, bb_ref, z_ref, laf_ref, lab_ref):
    xn = _rms(x_ref[...], g_ref[...]).astype(BF16)
    n_chunks = z_ref.shape[1] // PROJ_COLS
    for n in range(n_chunks):
        sl = slice(n * PROJ_COLS, (n + 1) * PROJ_COLS)
        acc = _dot(xn, w_ref[:, sl])
        z_ref[:, sl] = acc.astype(z_ref.dtype)
    c = acc[:, PROJ_COLS - LANE:].astype(BF16)
    laf_ref[...] = _log_sigmoid(_dot(c, w2f_ref[...]) + bf_ref[...]) * (1.0 / GLA_GATE_TEMP)
    lab_ref[...] = _log_sigmoid(_dot(c, w2b_ref[...]) + bb_ref[...]) * (1.0 / GLA_GATE_TEMP)


def _mixin(x, g, w, w2f, bf, w2b, bb):
    t, d = x.shape
    n = w.shape[1]
    tm = min(PROJ_ROWS, t)
    rows = lambda width: pl.BlockSpec((tm, width), lambda i: (i, 0))
    return pl.pallas_call(
        _mixin_kernel,
        grid=(t // tm,),
        in_specs=[rows(d), _resident((1, d)), _resident((d, n)),
                  _resident((LANE, GLA_QK)), _resident((1, GLA_QK)),
                  _resident((LANE, GLA_QK)), _resident((1, GLA_QK))],
        out_specs=[rows(n), rows(GLA_QK), rows(GLA_QK)],
        out_shape=[
            jax.ShapeDtypeStruct((t, n), BF16),
            jax.ShapeDtypeStruct((t, GLA_QK), F32),
            jax.ShapeDtypeStruct((t, GLA_QK), F32),
        ],
        compiler_params=_params("parallel"),
        name="mixin",
    )(x, g, w, w2f, bf, w2b, bb)


def _gla_kernel(q_ref, k_ref, v_ref, r_ref, laf_ref, lab_ref, gn_ref, o_ref,
                oacc_ref, sf_ref, sb_ref, *, seq):
    c = GLA_CHUNK
    blk = GLA_BLOCK
    cpb = blk // c
    n_blocks = seq // blk
    row = lax.broadcasted_iota(jnp.int32, (blk, blk), 0)
    col = lax.broadcasted_iota(jnp.int32, (blk, blk), 1)
    same_chunk = (row // c) == (col // c)
    lower = same_chunk & (col <= row)
    upper = same_chunk & (col >= row)
    row_in_chunk = lax.broadcasted_iota(jnp.int32, (blk, GLA_DK), 0) % c
    q_scale = GLA_DK ** -0.5

    sf_ref[...] = jnp.zeros_like(sf_ref)
    sb_ref[...] = jnp.zeros_like(sb_ref)

    def chunk_cumsum(x, forward):
        shift = 1
        while shift < c:
            if forward:
                x = x + jnp.where(row_in_chunk >= shift, pltpu.roll(x, shift, 0), 0.0)
            else:
                x = x + jnp.where(row_in_chunk < c - shift, pltpu.roll(x, blk - shift, 0), 0.0)
            shift *= 2
        return x

    def direction(rows, la_ref, mask, forward, st_ref):
        b = chunk_cumsum(la_ref[rows, :], forward)
        last = c - 1 if forward else 0
        b_end = [b[i * c + last:i * c + last + 1, :] for i in range(cpb)]
        bl = jnp.concatenate([jnp.broadcast_to(r, (c, GLA_DK)) for r in b_end], axis=0)
        b_t = [b[i * LANE:(i + 1) * LANE, :].T for i in range(blk // LANE)]
        q = q_ref[rows, :].astype(F32) * q_scale
        k = k_ref[rows, :].astype(F32)
        qt = (q * jnp.exp(b)).astype(BF16)
        kt = (k * jnp.exp(-b)).astype(BF16)
        ks = (k * jnp.exp(bl - b)).astype(BF16)
        v = v_ref[rows, :]
        a = jnp.where(mask, _dot_nt(qt, kt), 0.0).astype(BF16)
        o_intra = _dot(a, v)
        st = st_ref[...]
        outs = [None] * cpb
        for i in (range(cpb) if forward else reversed(range(cpb))):
            sl = slice(i * c, (i + 1) * c)
            end = (i * c + last) % LANE
            outs[i] = o_intra[sl] + _dot(qt[sl], st.astype(BF16))
            st = st * jnp.exp(b_t[i * c // LANE][:, end:end + 1]) + _dot_tn(ks[sl], v[sl])
        st_ref[...] = st
        return jnp.concatenate(outs, axis=0)

    def finish(rows, o):
        o = o * lax.rsqrt(jnp.mean(o * o, axis=-1, keepdims=True) + EPS) * gn_ref[...]
        r = r_ref[rows, :].astype(F32)
        o_ref[rows, :] = (o * (r * jax.nn.sigmoid(r))).astype(o_ref.dtype)

    def rows_of(n):
        return pl.ds(pl.multiple_of(n * blk, blk), blk)

    def first_half(n, carry):
        rf, rb = rows_of(n), rows_of(n_blocks - 1 - n)
        oacc_ref[rf, :] = direction(rf, laf_ref, lower, True, sf_ref)
        oacc_ref[rb, :] = direction(rb, lab_ref, upper, False, sb_ref)
        return carry

    def second_half(n, carry):
        rf, rb = rows_of(n), rows_of(n_blocks - 1 - n)
        finish(rf, oacc_ref[rf, :] + direction(rf, laf_ref, lower, True, sf_ref))
        finish(rb, oacc_ref[rb, :] + direction(rb, lab_ref, upper, False, sb_ref))
        return carry

    lax.fori_loop(0, n_blocks // 2, first_half, 0, unroll=4)
    lax.fori_loop(n_blocks // 2, n_blocks, second_half, 0, unroll=2)


def _gla(z, laf, lab, gn):
    b, s, _ = z.shape
    assert s % (2 * GLA_BLOCK) == 0
    k_blk = GLA_QK // GLA_DK
    v_blk = 2 * GLA_QK // GLA_DV
    r_blk = (2 * GLA_QK + GLA_V) // GLA_DV
    return pl.pallas_call(
        functools.partial(_gla_kernel, seq=s),
        grid=(b, GLA_HEADS),
        in_specs=[
            pl.BlockSpec((None, s, GLA_DK), lambda i, h: (i, 0, h)),
            pl.BlockSpec((None, s, GLA_DK), lambda i, h: (i, 0, k_blk + h)),
            pl.BlockSpec((None, s, GLA_DV), lambda i, h: (i, 0, v_blk + h)),
            pl.BlockSpec((None, s, GLA_DV), lambda i, h: (i, 0, r_blk + h)),
            pl.BlockSpec((None, s, GLA_DK), lambda i, h: (i, 0, h)),
            pl.BlockSpec((None, s, GLA_DK), lambda i, h: (i, 0, h)),
            pl.BlockSpec((1, GLA_DV), lambda i, h: (0, h)),
        ],
        out_specs=pl.BlockSpec((None, s, GLA_DV), lambda i, h: (i, 0, h)),
        out_shape=jax.ShapeDtypeStruct((b, s, GLA_V), BF16),
        scratch_shapes=[
            pltpu.VMEM((s, GLA_DV), F32),
            pltpu.VMEM((GLA_DK, GLA_DV), F32),
            pltpu.VMEM((GLA_DK, GLA_DV), F32),
        ],
        compiler_params=_params("parallel", "parallel"),
        name="gla",
    )(z, z, z, z, laf, lab, gn)


def _rope(x, cos, sin):
    half = MLA_ROPE // 2
    return x * cos - pltpu.roll(x, LANE - half, 1) * sin + pltpu.roll(x, half, 1) * sin


def _mla_prep_kernel(cq_ref, ckv_ref, kpe_ref, gq_ref, wuq_ref, gkv_ref, wukv_ref,
                     gqq_ref, gqk_ref, cos_ref, sin_ref, q_ref, k_ref, v_ref):
    cos = cos_ref[...]
    sin = sin_ref[...]
    inv_w = 1.0 / MLA_QK_HEAD
    scale = MLA_QK_HEAD ** -0.5 * LOG2_E
    hp = MLA_HEAD_PAD

    qn = _rms(cq_ref[...].astype(F32), gq_ref[...]).astype(BF16)
    qf = _dot(qn, wuq_ref[...])
    gqq = gqq_ref[...]
    for h in range(MLA_HEADS):
        qh = qf[:, h * hp:(h + 1) * hp]
        r = lax.rsqrt(jnp.sum(qh * qh, axis=-1, keepdims=True) * inv_w + EPS)
        qh = qh * r * gqq
        q_ref[h, :, :MLA_NOPE] = (qh[:, :MLA_NOPE] * scale).astype(BF16)
        q_ref[h, :, MLA_NOPE:] = (_rope(qh[:, MLA_NOPE:], cos, sin) * scale).astype(BF16)

    kvn = _rms(ckv_ref[...].astype(F32), gkv_ref[...]).astype(BF16)
    kvf = _dot(kvn, wukv_ref[...])
    kpe = kpe_ref[...].astype(F32)
    kpe_ss = jnp.sum(kpe * kpe, axis=-1, keepdims=True)
    gqk = gqk_ref[...]
    hw = MLA_NOPE + MLA_DV
    for h in range(MLA_HEADS):
        kn = kvf[:, h * hw:h * hw + MLA_NOPE]
        r = lax.rsqrt((jnp.sum(kn * kn, axis=-1, keepdims=True) + kpe_ss) * inv_w + EPS)
        k_ref[h, :, :MLA_NOPE] = (kn * r * gqk[:, :MLA_NOPE]).astype(BF16)
        k_ref[h, :, MLA_NOPE:] = _rope(kpe * r * gqk[:, MLA_NOPE:], cos, sin).astype(BF16)
        v_ref[h] = kvf[:, h * hw + MLA_NOPE:(h + 1) * hw].astype(BF16)


def _mla_prep(z, gq, wuq, gkv, wukv, gqq, gqk, cos, sin):
    b, s, _ = z.shape
    hp = MLA_HEAD_PAD
    tm = min(PROJ_ROWS, s)
    return pl.pallas_call(
        _mla_prep_kernel,
        grid=(b, s // tm),
        in_specs=[
            pl.BlockSpec((None, tm, MLA_Q_RANK), lambda i, j: (i, j, Z_CQ // MLA_Q_RANK)),
            pl.BlockSpec((None, tm, MLA_KV_RANK), lambda i, j: (i, j, Z_CKV // MLA_KV_RANK)),
            pl.BlockSpec((None, tm, LANE), lambda i, j: (i, j, Z_KPE // LANE)),
            _resident((1, MLA_Q_RANK)),
            _resident((MLA_Q_RANK, MLA_HEADS * hp)),
            _resident((1, MLA_KV_RANK)),
            _resident((MLA_KV_RANK, MLA_HEADS * (MLA_NOPE + MLA_DV))),
            _resident((1, hp)),
            _resident((1, hp)),
            pl.BlockSpec((tm, LANE), lambda i, j: (j, 0)),
            pl.BlockSpec((tm, LANE), lambda i, j: (j, 0)),
        ],
        out_specs=[
            pl.BlockSpec((None, MLA_HEADS, tm, hp), lambda i, j: (i, 0, j, 0)),
            pl.BlockSpec((None, MLA_HEADS, tm, hp), lambda i, j: (i, 0, j, 0)),
            pl.BlockSpec((None, MLA_HEADS, tm, MLA_DV), lambda i, j: (i, 0, j, 0)),
        ],
        out_shape=[
            jax.ShapeDtypeStruct((b, MLA_HEADS, s, hp), BF16),
            jax.ShapeDtypeStruct((b, MLA_HEADS, s, hp), BF16),
            jax.ShapeDtypeStruct((b, MLA_HEADS, s, MLA_DV), BF16),
        ],
        compiler_params=_params("parallel", "parallel"),
        name="mla_prep",
    )(z, z, z, gq, wuq, gkv, wukv, gqq, gqk, cos, sin)


def _flash_kernel(q_ref, k_ref, v_ref, o_ref, *, seq, tk):
    q = q_ref[...]
    m = l = acc = None
    for n in range(seq // tk):
        rows = pl.ds(n * tk, tk)
        s = _dot_nt(q, k_ref[rows, :])
        s_max = jnp.max(s, axis=-1, keepdims=True)
        if n == 0:
            m = s_max
            p = jnp.exp2(s - m)
            l = jnp.sum(p, axis=-1, keepdims=True)
            acc = _dot(p.astype(BF16), v_ref[rows, :])
        else:
            m_new = jnp.maximum(m, s_max)
            alpha = jnp.exp2(m - m_new)
            p = jnp.exp2(s - m_new)
            l = alpha * l + jnp.sum(p, axis=-1, keepdims=True)
            acc = alpha * acc + _dot(p.astype(BF16), v_ref[rows, :])
            m = m_new
    o_ref[...] = (acc / l).astype(o_ref.dtype)


def _flash(q, k, v):
    b, h, s, hp = q.shape
    tq = min(FLASH_Q, s)
    tk = min(FLASH_K, s)
    return pl.pallas_call(
        functools.partial(_flash_kernel, seq=s, tk=tk),
        grid=(b, h, s // tq),
        in_specs=[
            pl.BlockSpec((None, None, tq, hp), lambda i, j, n: (i, j, n, 0)),
            pl.BlockSpec((None, None, s, hp), lambda i, j, n: (i, j, 0, 0)),
            pl.BlockSpec((None, None, s, MLA_DV), lambda i, j, n: (i, j, 0, 0)),
        ],
        out_specs=pl.BlockSpec((None, tq, MLA_DV), lambda i, j, n: (i, n, j)),
        out_shape=jax.ShapeDtypeStruct((b, s, h * MLA_DV), BF16),
        compiler_params=_params("parallel", "parallel", "arbitrary"),
        name="mla_flash",
    )(q, k, v)


def _post1_kernel(a_ref, b_ref, res_ref, wa_ref, wb_ref, g_ref, wq_ref, hg_ref, h_ref, q_ref):
    h = res_ref[...] + _dot(a_ref[...], wa_ref[...]) + _dot(b_ref[...], wb_ref[...])
    h_ref[...] = h
    xn = _rms(h, g_ref[...]).astype(BF16)
    hg = hg_ref[...] * (MEM_DH ** -0.5 * LOG2_E)
    for n in range(q_ref.shape[1] // PROJ_COLS):
        sl = slice(n * PROJ_COLS, (n + 1) * PROJ_COLS)
        acc = _dot(xn, wq_ref[:, sl])
        for h2 in range(PROJ_COLS // MEM_DH):
            hs = slice(h2 * MEM_DH, (h2 + 1) * MEM_DH)
            q_ref[:, n * PROJ_COLS + h2 * MEM_DH:n * PROJ_COLS + (h2 + 1) * MEM_DH] = (
                _rms(acc[:, hs], hg).astype(q_ref.dtype))


def _post1(a, b, res, wa, wb, g, wq, hg):
    t, d = res.shape
    tm = min(PROJ_ROWS, t)
    rows = lambda width: pl.BlockSpec((tm, width), lambda i: (i, 0))
    return pl.pallas_call(
        _post1_kernel,
        grid=(t // tm,),
        in_specs=[rows(a.shape[1]), rows(b.shape[1]), rows(d),
                  _resident(wa.shape), _resident(wb.shape), _resident((1, d)),
                  _resident(wq.shape), _resident((1, MEM_DH))],
        out_specs=[rows(d), rows(d)],
        out_shape=[jax.ShapeDtypeStruct((t, d), F32), jax.ShapeDtypeStruct((t, d), BF16)],
        compiler_params=_params("parallel"),
        name="post1",
    )(a, b, res, wa, wb, g, wq, hg)


def _post2_kernel(q_ref, k_ref, v_ref, res_ref, wo_ref, o_ref):
    acc = res_ref[...]
    for h in range(MEM_HEADS):
        sl = slice(h * MEM_DH, (h + 1) * MEM_DH)
        s = _dot_nt(q_ref[:, sl], k_ref[:, sl])
        p = jnp.exp2(s - jnp.max(s, axis=-1, keepdims=True))
        l = jnp.sum(p, axis=-1, keepdims=True)
        oh = (_dot(p.astype(BF16), v_ref[:, sl]) / l).astype(BF16)
        acc = acc + _dot(oh, wo_ref[sl, :])
    o_ref[...] = acc


def _post2(q, k, v, res, wo):
    b, s, d = q.shape
    m = k.shape[1]
    tm = min(PROJ_ROWS, s)
    tok = pl.BlockSpec((None, tm, d), lambda i, j: (i, j, 0))
    mem = pl.BlockSpec((None, m, d), lambda i, j: (i, 0, 0))
    return pl.pallas_call(
        _post2_kernel,
        grid=(b, s // tm),
        in_specs=[tok, mem, mem, tok, _resident(wo.shape)],
        out_specs=tok,
        out_shape=jax.ShapeDtypeStruct((b, s, d), F32),
        compiler_params=_params("parallel", "arbitrary"),
        name="post2",
    )(q, k, v, res, wo)


def _memkv_kernel(x_ref, g_ref, wk_ref, wv_ref, hg_ref, k_ref, v_ref):
    xn = _rms(x_ref[...], g_ref[...]).astype(BF16)
    hg = hg_ref[...]
    for n in range(k_ref.shape[1] // PROJ_COLS):
        sl = slice(n * PROJ_COLS, (n + 1) * PROJ_COLS)
        acc = _dot(xn, wk_ref[:, sl])
        for h in range(PROJ_COLS // MEM_DH):
            hs = slice(h * MEM_DH, (h + 1) * MEM_DH)
            k_ref[:, n * PROJ_COLS + h * MEM_DH:n * PROJ_COLS + (h + 1) * MEM_DH] = (
                _rms(acc[:, hs], hg).astype(k_ref.dtype))
        v_ref[:, sl] = _dot(xn, wv_ref[:, sl]).astype(v_ref.dtype)


def _memkv(x, g, wk, wv, hg):
    t, d = x.shape
    tm = min(MEM_ROWS, t)
    rows = pl.BlockSpec((tm, d), lambda i: (i, 0))
    return pl.pallas_call(
        _memkv_kernel,
        grid=(t // tm,),
        in_specs=[rows, _resident((1, d)), _resident(wk.shape), _resident(wv.shape), _resident((1, MEM_DH))],
        out_specs=[rows, rows],
        out_shape=[jax.ShapeDtypeStruct((t, d), BF16), jax.ShapeDtypeStruct((t, d), BF16)],
        compiler_params=_params("parallel"),
        name="mem_kv",
    )(x, g, wk, wv, hg)


def _rope_tables(seq):
    half = MLA_ROPE // 2
    inv = ROPE_THETA ** (-jnp.arange(half, dtype=F32) / half)
    ang = jnp.arange(seq, dtype=jnp.int32).astype(F32)[:, None] * inv[None, :]
    pad = jnp.zeros((seq, LANE - MLA_ROPE), F32)
    cos = jnp.concatenate([jnp.cos(ang), jnp.cos(ang), pad], axis=-1)
    sin = jnp.concatenate([jnp.sin(ang), jnp.sin(ang), pad], axis=-1)
    return cos, sin


def _pad_cols(w, n):
    return jnp.concatenate([w, jnp.zeros((w.shape[0], n - w.shape[1]), w.dtype)], axis=1)


def _prepare(P):
    row = lambda a: a.reshape(1, -1).astype(F32)
    W = {}
    for pre in ("ffn1", "ffn2"):
        W[pre] = (row(P[pre + "_norm"]),
                  P[pre + "_w_gate"].astype(BF16), P[pre + "_w_up"].astype(BF16),
                  P[pre + "_w_down"].astype(BF16))
    w_in = P["w_in"].astype(BF16)
    g0, g1 = Z_GATE_START, Z_GATE_START + 2 * GLA_GATE_RANK
    rank = GLA_GATE_RANK
    W["mixin"] = (row(P["mix_norm"]),
                  jnp.concatenate([w_in[:, :g0], w_in[:, g1:],
                                   jnp.zeros((D_MODEL, Z_PAD - LANE - Z_MAIN), BF16),
                                   _pad_cols(w_in[:, g0:g1], LANE)], axis=1),
                  jnp.pad(P["gla_wa2_fwd"].astype(BF16), ((0, LANE - rank), (0, 0))), row(P["gla_ba_fwd"]),
                  jnp.pad(P["gla_wa2_bwd"].astype(BF16), ((rank, LANE - 2 * rank), (0, 0))), row(P["gla_ba_bwd"]))
    W["gla_norm"] = row(P["gla_out_norm"])
    hp = MLA_HEAD_PAD
    wuq = P["mla_w_uq"].astype(BF16).reshape(MLA_Q_RANK, MLA_HEADS, MLA_QK_HEAD)
    wuq = jnp.pad(wuq, ((0, 0), (0, 0), (0, hp - MLA_QK_HEAD))).reshape(MLA_Q_RANK, MLA_HEADS * hp)
    W["mla"] = (row(P["mla_q_norm"]), wuq, row(P["mla_kv_norm"]), P["mla_w_ukv"].astype(BF16),
                _pad_cols(row(P["mla_qk_q_norm"]), hp), _pad_cols(row(P["mla_qk_k_norm"]), hp))
    w_out = P["w_out"].astype(BF16)
    W["post1"] = (w_out[:GLA_V], w_out[GLA_V:], row(P["xattn_norm"]), P["xattn_wq"].astype(BF16),
                  row(P["xattn_q_norm"]))
    W["memkv"] = (row(P["mem_norm"]), P["xattn_wk"].astype(BF16), P["xattn_wv"].astype(BF16),
                  row(P["xattn_k_norm"]))
    W["wo"] = P["xattn_wo"].astype(BF16)
    return W


def _layer(x, mem, W):
    b, s, d = x.shape
    m = mem.shape[1]
    t = b * s

    h = _ffn(x.reshape(t, d), *W["ffn1"])

    z, laf, lab = _mixin(h, *W["mixin"])
    z3 = z.reshape(b, s, Z_PAD)
    y_gla = _gla(z3, laf.reshape(b, s, GLA_QK), lab.reshape(b, s, GLA_QK), W["gla_norm"])
    q, k, v = _mla_prep(z3, *W["mla"], *_rope_tables(s))
    y_mla = _flash(q, k, v)

    h, xq = _post1(y_gla.reshape(t, GLA_V), y_mla.reshape(t, MLA_V), h, *W["post1"])
    xk, xv = _memkv(mem.reshape(b * m, d), *W["memkv"])
    h = _post2(xq.reshape(b, s, d), xk.reshape(b, m, d), xv.reshape(b, m, d), h.reshape(b, s, d), W["wo"])

    h = _ffn(h.reshape(t, d), *W["ffn2"])
    return h.reshape(b, s, d)


def kernel(x_prompt, x_sample, mem_prompt, mem_sample, ffn1_norm, ffn1_w_gate, ffn1_w_up, ffn1_w_down, mix_norm, w_in, gla_wa2_fwd, gla_ba_fwd, gla_wa2_bwd, gla_ba_bwd, gla_out_norm, mla_q_norm, mla_w_uq, mla_kv_norm, mla_w_ukv, mla_qk_q_norm, mla_qk_k_norm, w_out, xattn_norm, mem_norm, xattn_wq, xattn_wk, xattn_wv, xattn_q_norm, xattn_k_norm, xattn_wo, ffn2_norm, ffn2_w_gate, ffn2_w_up, ffn2_w_down):
    stacked = dict(
        ffn1_norm=ffn1_norm, ffn1_w_gate=ffn1_w_gate, ffn1_w_up=ffn1_w_up, ffn1_w_down=ffn1_w_down,
        mix_norm=mix_norm, w_in=w_in,
        gla_wa2_fwd=gla_wa2_fwd, gla_ba_fwd=gla_ba_fwd, gla_wa2_bwd=gla_wa2_bwd, gla_ba_bwd=gla_ba_bwd,
        gla_out_norm=gla_out_norm,
        mla_q_norm=mla_q_norm, mla_w_uq=mla_w_uq, mla_kv_norm=mla_kv_norm, mla_w_ukv=mla_w_ukv,
        mla_qk_q_norm=mla_qk_q_norm, mla_qk_k_norm=mla_qk_k_norm,
        w_out=w_out,
        xattn_norm=xattn_norm, mem_norm=mem_norm, xattn_wq=xattn_wq, xattn_wk=xattn_wk, xattn_wv=xattn_wv,
        xattn_q_norm=xattn_q_norm, xattn_k_norm=xattn_k_norm, xattn_wo=xattn_wo,
        ffn2_norm=ffn2_norm, ffn2_w_gate=ffn2_w_gate, ffn2_w_up=ffn2_w_up, ffn2_w_down=ffn2_w_down,
    )
    y_prompt, y_sample = x_prompt, x_sample
    for l in range(ffn1_norm.shape[0]):
        W = _prepare({name: a[l] for name, a in stacked.items()})
        y_prompt = _layer(y_prompt, mem_prompt, W)
        y_sample = _layer(y_sample, mem_sample, W)
    return (y_prompt, y_sample)
```

```python
import functools

import jax
import jax.numpy as jnp
from jax import lax
from jax.experimental import pallas as pl
from jax.experimental.pallas import tpu as pltpu

F32 = jnp.float32
BF16 = jnp.bfloat16

D_MODEL = 2048
D_FF = 5504
MEM_HEADS = 4
MEM_DH = D_MODEL // MEM_HEADS
GLA_HEADS = 4
GLA_DK = 128
GLA_DV = 256
GLA_GATE_RANK = 16
GLA_GATE_TEMP = 16.0
GLA_CHUNK = 64
MLA_HEADS = 8
MLA_Q_RANK = 512
MLA_KV_RANK = 256
MLA_NOPE = 128
MLA_ROPE = 64
MLA_DV = 128
ROPE_THETA = 10000.0
EPS = 1e-6
LOG2_E = 1.4426950408889634

GLA_QK = GLA_HEADS * GLA_DK
GLA_V = GLA_HEADS * GLA_DV
MLA_V = MLA_HEADS * MLA_DV
MLA_QK_HEAD = MLA_NOPE + MLA_ROPE

LANE = 128
MXU_DIM = 256
VMEM_LIMIT = 60 * 1024 * 1024

MLA_HEAD_PAD = MXU_DIM
GLA_BLOCK = MXU_DIM
D_FF_TILE = 512
FFN_ROWS = 1024
PROJ_ROWS = 512
MEM_ROWS = 256
PROJ_COLS = 1024
FLASH_Q = 1024
FLASH_K = 1024
Z_GATE_START = 2 * GLA_QK + 2 * GLA_V
Z_MAIN = Z_GATE_START + MLA_Q_RANK + MLA_KV_RANK + MLA_ROPE
Z_PAD = -(-(Z_MAIN + LANE) // PROJ_COLS) * PROJ_COLS
Z_CQ = Z_GATE_START
Z_CKV = Z_CQ + MLA_Q_RANK
Z_KPE = Z_CKV + MLA_KV_RANK


def _params(*sem):
    return pltpu.CompilerParams(dimension_semantics=sem, vmem_limit_bytes=VMEM_LIMIT)


def _resident(shape):
    return pl.BlockSpec(shape, lambda *_: (0,) * len(shape), pipeline_mode=pl.Buffered(1))


def _rms(x, g):
    return x * lax.rsqrt(jnp.mean(x * x, axis=-1, keepdims=True) + EPS) * g


def _dot(a, b):
    return jnp.dot(a, b, preferred_element_type=F32)


def _dot_nt(a, b):
    return lax.dot_general(a, b, (((1,), (1,)), ((), ())), preferred_element_type=F32)


def _dot_tn(a, b):
    return lax.dot_general(a, b, (((0,), (0,)), ((), ())), preferred_element_type=F32)


def _ffn_kernel(x_ref, g_ref, wg_ref, wu_ref, wd_ref, o_ref, xn_ref, *, overlap):
    j = pl.program_id(1)
    last = pl.num_programs(1) - 1

    @pl.when(j == 0)
    def _():
        x = x_ref[...]
        xn_ref[...] = _rms(x, g_ref[...]).astype(BF16)
        o_ref[...] = x

    xn = xn_ref[...]
    gate = _dot(xn, wg_ref[...])
    up = _dot(xn, wu_ref[...])
    hid = gate * jax.nn.sigmoid(gate) * up
    if overlap:
        col = lax.broadcasted_iota(jnp.int32, (1, hid.shape[1]), 1)
        hid = jnp.where((col >= overlap) | (j != last), hid, 0.0)
    o_ref[...] += _dot(hid.astype(BF16), wd_ref[...])


def _ffn(x, g, wg, wu, wd):
    t, d = x.shape
    f = wg.shape[1]
    tm = min(FFN_ROWS, t)
    tf = D_FF_TILE
    nj = -(-f // tf)
    assert f % LANE == 0 and tf % LANE == 0
    start = lambda j: jnp.minimum(j * (tf // LANE), (f - tf) // LANE) * LANE
    return pl.pallas_call(
        functools.partial(_ffn_kernel, overlap=nj * tf - f),
        grid=(t // tm, nj),
        in_specs=[
            pl.BlockSpec((tm, d), lambda i, j: (i, 0)),
            pl.BlockSpec((1, d), lambda i, j: (0, 0)),
            pl.BlockSpec((pl.Element(d), pl.Element(tf)), lambda i, j: (0, start(j))),
            pl.BlockSpec((pl.Element(d), pl.Element(tf)), lambda i, j: (0, start(j))),
            pl.BlockSpec((pl.Element(tf), pl.Element(d)), lambda i, j: (start(j), 0)),
        ],
        out_specs=pl.BlockSpec((tm, d), lambda i, j: (i, 0)),
        out_shape=jax.ShapeDtypeStruct((t, d), F32),
        scratch_shapes=[pltpu.VMEM((tm, d), BF16)],
        compiler_params=_params("parallel", "arbitrary"),
        name="ffn",
    )(x, g, wg, wu, wd)


def _log_sigmoid(x):
    return jnp.minimum(x, 0.0) - jnp.log(1.0 + jnp.exp(-jnp.abs(x)))


def _mixin_kernel(x_ref, g_ref, w_ref, w2f_ref, bf_ref, w2b_ref, bb_ref, z_ref, laf_ref, lab_ref):
    xn = _rms(x_ref[...], g_ref[...]).astype(BF16)
    n_chunks = z_ref.shape[1] // PROJ_COLS
    for n in (n_chunks - 1, *range(n_chunks - 1)):
        sl = slice(n * PROJ_COLS, (n + 1) * PROJ_COLS)
        acc = _dot(xn, w_ref[:, sl])
        z_ref[:, sl] = acc.astype(z_ref.dtype)
        if n == n_chunks - 1:
            c = acc[:, PROJ_COLS - LANE:].astype(BF16)
            laf_ref[...] = _log_sigmoid(_dot(c, w2f_ref[...]) + bf_ref[...]) * (1.0 / GLA_GATE_TEMP)
            lab_ref[...] = _log_sigmoid(_dot(c, w2b_ref[...]) + bb_ref[...]) * (1.0 / GLA_GATE_TEMP)


def _mixin(x, g, w, w2f, bf, w2b, bb):
    t, d = x.shape
    n = w.shape[1]
    tm = min(PROJ_ROWS, t)
    rows = lambda width: pl.BlockSpec((tm, width), lambda i: (i, 0))
    return pl.pallas_call(
        _mixin_kernel,
        grid=(t // tm,),
        in_specs=[rows(d), _resident((1, d)), _resident((d, n)),
                  _resident((LANE, GLA_QK)), _resident((1, GLA_QK)),
                  _resident((LANE, GLA_QK)), _resident((1, GLA_QK))],
        out_specs=[rows(n), rows(GLA_QK), rows(GLA_QK)],
        out_shape=[
            jax.ShapeDtypeStruct((t, n), BF16),
            jax.ShapeDtypeStruct((t, GLA_QK), F32),
            jax.ShapeDtypeStruct((t, GLA_QK), F32),
        ],
        compiler_params=_params("parallel"),
        name="mixin",
    )(x, g, w, w2f, bf, w2b, bb)


def _gla_kernel(q_ref, k_ref, v_ref, r_ref, laf_ref, lab_ref, gn_ref, o_ref,
                oacc_ref, sf_ref, sb_ref, *, seq):
    c = GLA_CHUNK
    blk = GLA_BLOCK
    cpb = blk // c
    n_blocks = seq // blk
    row = lax.broadcasted_iota(jnp.int32, (blk, blk), 0)
    col = lax.broadcasted_iota(jnp.int32, (blk, blk), 1)
    same_chunk = (row // c) == (col // c)
    lower = same_chunk & (col <= row)
    upper = same_chunk & (col >= row)
    row_in_chunk = lax.broadcasted_iota(jnp.int32, (blk, GLA_DK), 0) % c
    q_scale = GLA_DK ** -0.5

    sf_ref[...] = jnp.zeros_like(sf_ref)
    sb_ref[...] = jnp.zeros_like(sb_ref)

    def chunk_cumsum(x, forward):
        shift = 1
        while shift < c:
            if forward:
                x = x + jnp.where(row_in_chunk >= shift, pltpu.roll(x, shift, 0), 0.0)
            else:
                x = x + jnp.where(row_in_chunk < c - shift, pltpu.roll(x, blk - shift, 0), 0.0)
            shift *= 2
        return x

    def direction(rows, la_ref, mask, forward, st_ref):
        b = chunk_cumsum(la_ref[rows, :], forward)
        last = c - 1 if forward else 0
        b_end = [b[i * c + last:i * c + last + 1, :] for i in range(cpb)]
        bl = jnp.concatenate([jnp.broadcast_to(r, (c, GLA_DK)) for r in b_end], axis=0)
        b_t = [b[i * LANE:(i + 1) * LANE, :].T for i in range(blk // LANE)]
        q = q_ref[rows, :].astype(F32) * q_scale
        k = k_ref[rows, :].astype(F32)
        qt = (q * jnp.exp(b)).astype(BF16)
        kt = (k * jnp.exp(-b)).astype(BF16)
        ks = (k * jnp.exp(bl - b)).astype(BF16)
        v = v_ref[rows, :]
        a = jnp.where(mask, _dot_nt(qt, kt), 0.0).astype(BF16)
        o_intra = _dot(a, v)
        st = st_ref[...]
        outs = [None] * cpb
        for i in (range(cpb) if forward else reversed(range(cpb))):
            sl = slice(i * c, (i + 1) * c)
            end = (i * c + last) % LANE
            outs[i] = o_intra[sl] + _dot(qt[sl], st.astype(BF16))
            st = st * jnp.exp(b_t[i * c // LANE][:, end:end + 1]) + _dot_tn(ks[sl], v[sl])
        st_ref[...] = st
        return jnp.concatenate(outs, axis=0)

    def finish(rows, o):
        o = o * lax.rsqrt(jnp.mean(o * o, axis=-1, keepdims=True) + EPS) * gn_ref[...]
        r = r_ref[rows, :].astype(F32)
        o_ref[rows, :] = (o * (r * jax.nn.sigmoid(r))).astype(o_ref.dtype)

    def rows_of(n):
        return pl.ds(pl.multiple_of(n * blk, blk), blk)

    def first_half(n, carry):
        rf, rb = rows_of(n), rows_of(n_blocks - 1 - n)
        oacc_ref[rf, :] = direction(rf, laf_ref, lower, True, sf_ref)
        oacc_ref[rb, :] = direction(rb, lab_ref, upper, False, sb_ref)
        return carry

    def second_half(n, carry):
        rf, rb = rows_of(n), rows_of(n_blocks - 1 - n)
        finish(rf, oacc_ref[rf, :] + direction(rf, laf_ref, lower, True, sf_ref))
        finish(rb, oacc_ref[rb, :] + direction(rb, lab_ref, upper, False, sb_ref))
        return carry

    lax.fori_loop(0, n_blocks // 2, first_half, 0, unroll=4)
    lax.fori_loop(n_blocks // 2, n_blocks, second_half, 0, unroll=2)


def _gla(z, laf, lab, gn):
    b, s, _ = z.shape
    assert s % (2 * GLA_BLOCK) == 0
    k_blk = GLA_QK // GLA_DK
    v_blk = 2 * GLA_QK // GLA_DV
    r_blk = (2 * GLA_QK + GLA_V) // GLA_DV
    return pl.pallas_call(
        functools.partial(_gla_kernel, seq=s),
        grid=(b, GLA_HEADS),
        in_specs=[
            pl.BlockSpec((None, s, GLA_DK), lambda i, h: (i, 0, h)),
            pl.BlockSpec((None, s, GLA_DK), lambda i, h: (i, 0, k_blk + h)),
            pl.BlockSpec((None, s, GLA_DV), lambda i, h: (i, 0, v_blk + h)),
            pl.BlockSpec((None, s, GLA_DV), lambda i, h: (i, 0, r_blk + h)),
            pl.BlockSpec((None, s, GLA_DK), lambda i, h: (i, 0, h)),
            pl.BlockSpec((None, s, GLA_DK), lambda i, h: (i, 0, h)),
            pl.BlockSpec((1, GLA_DV), lambda i, h: (0, h)),
        ],
        out_specs=pl.BlockSpec((None, s, GLA_DV), lambda i, h: (i, 0, h)),
        out_shape=jax.ShapeDtypeStruct((b, s, GLA_V), BF16),
        scratch_shapes=[
            pltpu.VMEM((s, GLA_DV), F32),
            pltpu.VMEM((GLA_DK, GLA_DV), F32),
            pltpu.VMEM((GLA_DK, GLA_DV), F32),
        ],
        compiler_params=_params("parallel", "parallel"),
        name="gla",
    )(z, z, z, z, laf, lab, gn)


def _rope(x, cos, sin):
    half = MLA_ROPE // 2
    return x * cos - pltpu.roll(x, LANE - half, 1) * sin + pltpu.roll(x, half, 1) * sin


def _mla_prep_kernel(cq_ref, ckv_ref, kpe_ref, gq_ref, wuq_ref, gkv_ref, wukv_ref,
                     gqq_ref, gqk_ref, cos_ref, sin_ref, q_ref, k_ref, v_ref):
    cos = cos_ref[...]
    sin = sin_ref[...]
    inv_w = 1.0 / MLA_QK_HEAD
    scale = MLA_QK_HEAD ** -0.5 * LOG2_E
    hp = MLA_HEAD_PAD

    qn = _rms(cq_ref[...].astype(F32), gq_ref[...]).astype(BF16)
    qf = _dot(qn, wuq_ref[...])
    gqq = gqq_ref[...] * scale
    for h in range(MLA_HEADS):
        qh = qf[:, h * hp:(h + 1) * hp]
        r = lax.rsqrt(jnp.sum(qh * qh, axis=-1, keepdims=True) * inv_w + EPS)
        q_ref[h, :, :MLA_NOPE] = (qh[:, :MLA_NOPE] * gqq[:, :MLA_NOPE] * r).astype(BF16)
        q_ref[h, :, MLA_NOPE:] = (_rope(qh[:, MLA_NOPE:] * gqq[:, MLA_NOPE:], cos, sin) * r).astype(BF16)

    kvn = _rms(ckv_ref[...].astype(F32), gkv_ref[...]).astype(BF16)
    kvf = _dot(kvn, wukv_ref[...])
    kpe = kpe_ref[...].astype(F32)
    kpe_ss = jnp.sum(kpe * kpe, axis=-1, keepdims=True)
    gqk = gqk_ref[...]
    k_rope = _rope(kpe * gqk[:, MLA_NOPE:], cos, sin)
    hw = MLA_NOPE + MLA_DV
    for h in range(MLA_HEADS):
        kn = kvf[:, h * hw:h * hw + MLA_NOPE]
        r = lax.rsqrt((jnp.sum(kn * kn, axis=-1, keepdims=True) + kpe_ss) * inv_w + EPS)
        k_ref[h, :, :MLA_NOPE] = (kn * gqk[:, :MLA_NOPE] * r).astype(BF16)
        k_ref[h, :, MLA_NOPE:] = (k_rope * r).astype(BF16)
        v_ref[h] = kvf[:, h * hw + MLA_NOPE:(h + 1) * hw].T.astype(BF16)


def _mla_prep(z, gq, wuq, gkv, wukv, gqq, gqk, cos, sin):
    b, s, _ = z.shape
    hp = MLA_HEAD_PAD
    tm = min(PROJ_ROWS, s)
    return pl.pallas_call(
        _mla_prep_kernel,
        grid=(b, s // tm),
        in_specs=[
            pl.BlockSpec((None, tm, MLA_Q_RANK), lambda i, j: (i, j, Z_CQ // MLA_Q_RANK)),
            pl.BlockSpec((None, tm, MLA_KV_RANK), lambda i, j: (i, j, Z_CKV // MLA_KV_RANK)),
            pl.BlockSpec((None, tm, LANE), lambda i, j: (i, j, Z_KPE // LANE)),
            _resident((1, MLA_Q_RANK)),
            _resident((MLA_Q_RANK, MLA_HEADS * hp)),
            _resident((1, MLA_KV_RANK)),
            _resident((MLA_KV_RANK, MLA_HEADS * (MLA_NOPE + MLA_DV))),
            _resident((1, hp)),
            _resident((1, hp)),
            pl.BlockSpec((tm, LANE), lambda i, j: (j, 0)),
            pl.BlockSpec((tm, LANE), lambda i, j: (j, 0)),
        ],
        out_specs=[
            pl.BlockSpec((None, MLA_HEADS, tm, hp), lambda i, j: (i, 0, j, 0)),
            pl.BlockSpec((None, MLA_HEADS, tm, hp), lambda i, j: (i, 0, j, 0)),
            pl.BlockSpec((None, MLA_HEADS, MLA_DV, tm), lambda i, j: (i, 0, 0, j)),
        ],
        out_shape=[
            jax.ShapeDtypeStruct((b, MLA_HEADS, s, hp), BF16),
            jax.ShapeDtypeStruct((b, MLA_HEADS, s, hp), BF16),
            jax.ShapeDtypeStruct((b, MLA_HEADS, MLA_DV, s), BF16),
        ],
        compiler_params=_params("parallel", "parallel"),
        name="mla_prep",
    )(z, z, z, gq, wuq, gkv, wukv, gqq, gqk, cos, sin)


def _flash_kernel(q_ref, k_ref, vt_ref, o_ref, *, seq, tk):
    q = q_ref[...]
    scores = lambda n: _dot_nt(k_ref[pl.ds(n * tk, tk), :], q)
    n_chunks = seq // tk
    m = l = acc = None
    s_next = scores(0)
    for n in range(n_chunks):
        s = s_next
        if n + 1 < n_chunks:
            s_next = scores(n + 1)
        vt = vt_ref[:, pl.ds(n * tk, tk)]
        s_max = jnp.max(s, axis=0, keepdims=True)
        if n == 0:
            m = s_max
            p = jnp.exp2(s - m)
            l = jnp.sum(p, axis=0, keepdims=True)
            acc = _dot(vt, p.astype(BF16))
        else:
            m_new = jnp.maximum(m, s_max)
            alpha = jnp.exp2(m - m_new)
            p = jnp.exp2(s - m_new)
            l = alpha * l + jnp.sum(p, axis=0, keepdims=True)
            acc = alpha * acc + _dot(vt, p.astype(BF16))
            m = m_new
    o_ref[...] = (acc / l).T.astype(o_ref.dtype)


def _flash(q, k, v):
    b, h, s, hp = q.shape
    tq = min(FLASH_Q, s)
    tk = min(FLASH_K, s)
    return pl.pallas_call(
        functools.partial(_flash_kernel, seq=s, tk=tk),
        grid=(b, h, s // tq),
        in_specs=[
            pl.BlockSpec((None, None, tq, hp), lambda i, j, n: (i, j, n, 0)),
            pl.BlockSpec((None, None, s, hp), lambda i, j, n: (i, j, 0, 0)),
            pl.BlockSpec((None, None, MLA_DV, s), lambda i, j, n: (i, j, 0, 0)),
        ],
        out_specs=pl.BlockSpec((None, tq, MLA_DV), lambda i, j, n: (i, n, j)),
        out_shape=jax.ShapeDtypeStruct((b, s, h * MLA_DV), BF16),
        compiler_params=_params("parallel", "parallel", "arbitrary"),
        name="mla_flash",
    )(q, k, v)


def _post1_kernel(a_ref, b_ref, res_ref, wa_ref, wb_ref, g_ref, wq_ref, hg_ref, h_ref, q_ref):
    h = res_ref[...] + _dot(a_ref[...], wa_ref[...]) + _dot(b_ref[...], wb_ref[...])
    h_ref[...] = h
    xn = _rms(h, g_ref[...]).astype(BF16)
    hg = hg_ref[...] * (MEM_DH ** -0.5 * LOG2_E)
    for n in range(q_ref.shape[1] // PROJ_COLS):
        sl = slice(n * PROJ_COLS, (n + 1) * PROJ_COLS)
        acc = _dot(xn, wq_ref[:, sl])
        for h2 in range(PROJ_COLS // MEM_DH):
            hs = slice(h2 * MEM_DH, (h2 + 1) * MEM_DH)
            q_ref[:, n * PROJ_COLS + h2 * MEM_DH:n * PROJ_COLS + (h2 + 1) * MEM_DH] = (
                _rms(acc[:, hs], hg).astype(q_ref.dtype))


def _post1(a, b, res, wa, wb, g, wq, hg):
    t, d = res.shape
    tm = min(PROJ_ROWS, t)
    rows = lambda width: pl.BlockSpec((tm, width), lambda i: (i, 0))
    return pl.pallas_call(
        _post1_kernel,
        grid=(t // tm,),
        in_specs=[rows(a.shape[1]), rows(b.shape[1]), rows(d),
                  _resident(wa.shape), _resident(wb.shape), _resident((1, d)),
                  _resident(wq.shape), _resident((1, MEM_DH))],
        out_specs=[rows(d), rows(d)],
        out_shape=[jax.ShapeDtypeStruct((t, d), F32), jax.ShapeDtypeStruct((t, d), BF16)],
        compiler_params=_params("parallel"),
        name="post1",
    )(a, b, res, wa, wb, g, wq, hg)


def _post2_kernel(q_ref, k_ref, v_ref, res_ref, wo_ref, o_ref, att_ref):
    for h in range(MEM_HEADS):
        sl = slice(h * MEM_DH, (h + 1) * MEM_DH)
        s = _dot_nt(q_ref[:, sl], k_ref[:, sl])
        p = jnp.exp2(s - jnp.max(s, axis=-1, keepdims=True))
        l = jnp.sum(p, axis=-1, keepdims=True)
        att_ref[:, sl] = (_dot(p.astype(BF16), v_ref[:, sl]) / l).astype(BF16)
    o_ref[...] = res_ref[...] + _dot(att_ref[...], wo_ref[...])


def _post2(q, k, v, res, wo):
    b, s, d = q.shape
    m = k.shape[1]
    tm = min(PROJ_ROWS, s)
    tok = pl.BlockSpec((None, tm, d), lambda i, j: (i, j, 0))
    mem = pl.BlockSpec((None, m, d), lambda i, j: (i, 0, 0))
    return pl.pallas_call(
        _post2_kernel,
        grid=(b, s // tm),
        in_specs=[tok, mem, mem, tok, _resident(wo.shape)],
        out_specs=tok,
        out_shape=jax.ShapeDtypeStruct((b, s, d), F32),
        scratch_shapes=[pltpu.VMEM((tm, d), BF16)],
        compiler_params=_params("parallel", "arbitrary"),
        name="post2",
    )(q, k, v, res, wo)


def _memkv_kernel(x_ref, g_ref, wk_ref, wv_ref, hg_ref, k_ref, v_ref):
    xn = _rms(x_ref[...], g_ref[...]).astype(BF16)
    hg = hg_ref[...]
    for n in range(k_ref.shape[1] // PROJ_COLS):
        sl = slice(n * PROJ_COLS, (n + 1) * PROJ_COLS)
        acc = _dot(xn, wk_ref[:, sl])
        for h in range(PROJ_COLS // MEM_DH):
            hs = slice(h * MEM_DH, (h + 1) * MEM_DH)
            k_ref[:, n * PROJ_COLS + h * MEM_DH:n * PROJ_COLS + (h + 1) * MEM_DH] = (
                _rms(acc[:, hs], hg).astype(k_ref.dtype))
        v_ref[:, sl] = _dot(xn, wv_ref[:, sl]).astype(v_ref.dtype)


def _memkv(x, g, wk, wv, hg):
    t, d = x.shape
    tm = min(MEM_ROWS, t)
    rows = pl.BlockSpec((tm, d), lambda i: (i, 0))
    return pl.pallas_call(
        _memkv_kernel,
        grid=(t // tm,),
        in_specs=[rows, _resident((1, d)), _resident(wk.shape), _resident(wv.shape), _resident((1, MEM_DH))],
        out_specs=[rows, rows],
        out_shape=[jax.ShapeDtypeStruct((t, d), BF16), jax.ShapeDtypeStruct((t, d), BF16)],
        compiler_params=_params("parallel"),
        name="mem_kv",
    )(x, g, wk, wv, hg)


def _rope_tables(seq):
    half = MLA_ROPE // 2
    inv = ROPE_THETA ** (-jnp.arange(half, dtype=F32) / half)
    ang = jnp.arange(seq, dtype=jnp.int32).astype(F32)[:, None] * inv[None, :]
    pad = jnp.zeros((seq, LANE - MLA_ROPE), F32)
    cos = jnp.concatenate([jnp.cos(ang), jnp.cos(ang), pad], axis=-1)
    sin = jnp.concatenate([jnp.sin(ang), jnp.sin(ang), pad], axis=-1)
    return cos, sin


def _pad_cols(w, n):
    return jnp.concatenate([w, jnp.zeros((w.shape[0], n - w.shape[1]), w.dtype)], axis=1)


def _prepare(P):
    row = lambda a: a.reshape(1, -1).astype(F32)
    W = {}
    for pre in ("ffn1", "ffn2"):
        W[pre] = (row(P[pre + "_norm"]),
                  P[pre + "_w_gate"].astype(BF16), P[pre + "_w_up"].astype(BF16),
                  (0.5 * P[pre + "_w_down"]).astype(BF16))
    w_in = P["w_in"].astype(BF16)
    g0, g1 = Z_GATE_START, Z_GATE_START + 2 * GLA_GATE_RANK
    rank = GLA_GATE_RANK
    W["mixin"] = (row(P["mix_norm"]),
                  jnp.concatenate([w_in[:, :g0], w_in[:, g1:],
                                   jnp.zeros((D_MODEL, Z_PAD - LANE - Z_MAIN), BF16),
                                   _pad_cols(w_in[:, g0:g1], LANE)], axis=1),
                  jnp.pad(P["gla_wa2_fwd"].astype(BF16), ((0, LANE - rank), (0, 0))), row(P["gla_ba_fwd"]),
                  jnp.pad(P["gla_wa2_bwd"].astype(BF16), ((rank, LANE - 2 * rank), (0, 0))), row(P["gla_ba_bwd"]))
    W["gla_norm"] = row(P["gla_out_norm"])
    hp = MLA_HEAD_PAD
    wuq = P["mla_w_uq"].astype(BF16).reshape(MLA_Q_RANK, MLA_HEADS, MLA_QK_HEAD)
    wuq = jnp.pad(wuq, ((0, 0), (0, 0), (0, hp - MLA_QK_HEAD))).reshape(MLA_Q_RANK, MLA_HEADS * hp)
    W["mla"] = (row(P["mla_q_norm"]), wuq, row(P["mla_kv_norm"]), P["mla_w_ukv"].astype(BF16),
                _pad_cols(row(P["mla_qk_q_norm"]), hp), _pad_cols(row(P["mla_qk_k_norm"]), hp))
    w_out = P["w_out"].astype(BF16)
    W["post1"] = (w_out[:GLA_V], w_out[GLA_V:], row(P["xattn_norm"]), P["xattn_wq"].astype(BF16),
                  row(P["xattn_q_norm"]))
    W["memkv"] = (row(P["mem_norm"]), P["xattn_wk"].astype(BF16), P["xattn_wv"].astype(BF16),
                  row(P["xattn_k_norm"]))
    W["wo"] = P["xattn_wo"].astype(BF16)
    return W


def _layer(x, mem, W):
    b, s, d = x.shape
    m = mem.shape[1]
    t = b * s

    h = _ffn(x.reshape(t, d), *W["ffn1"])

    z, laf, lab = _mixin(h, *W["mixin"])
    z3 = z.reshape(b, s, Z_PAD)
    y_gla = _gla(z3, laf.reshape(b, s, GLA_QK), lab.reshape(b, s, GLA_QK), W["gla_norm"])
    q, k, v = _mla_prep(z3, *W["mla"], *_rope_tables(s))
    y_mla = _flash(q, k, v)

    h, xq = _post1(y_gla.reshape(t, GLA_V), y_mla.reshape(t, MLA_V), h, *W["post1"])
    xk, xv = _memkv(mem.reshape(b * m, d), *W["memkv"])
    h = _post2(xq.reshape(b, s, d), xk.reshape(b, m, d), xv.reshape(b, m, d), h.reshape(b, s, d), W["wo"])

    h = _ffn(h.reshape(t, d), *W["ffn2"])
    return h.reshape(b, s, d)


def kernel(x_prompt, x_sample, mem_prompt, mem_sample, ffn1_norm, ffn1_w_gate, ffn1_w_up, ffn1_w_down, mix_norm, w_in, gla_wa2_fwd, gla_ba_fwd, gla_wa2_bwd, gla_ba_bwd, gla_out_norm, mla_q_norm, mla_w_uq, mla_kv_norm, mla_w_ukv, mla_qk_q_norm, mla_qk_k_norm, w_out, xattn_norm, mem_norm, xattn_wq, xattn_wk, xattn_wv, xattn_q_norm, xattn_k_norm, xattn_wo, ffn2_norm, ffn2_w_gate, ffn2_w_up, ffn2_w_down):
    stacked = dict(
        ffn1_norm=ffn1_norm, ffn1_w_gate=ffn1_w_gate, ffn1_w_up=ffn1_w_up, ffn1_w_down=ffn1_w_down,
        mix_norm=mix_norm, w_in=w_in,
        gla_wa2_fwd=gla_wa2_fwd, gla_ba_fwd=gla_ba_fwd, gla_wa2_bwd=gla_wa2_bwd, gla_ba_bwd=gla_ba_bwd,
        gla_out_norm=gla_out_norm,
        mla_q_norm=mla_q_norm, mla_w_uq=mla_w_uq, mla_kv_norm=mla_kv_norm, mla_w_ukv=mla_w_ukv,
        mla_qk_q_norm=mla_qk_q_norm, mla_qk_k_norm=mla_qk_k_norm,
        w_out=w_out,
        xattn_norm=xattn_norm, mem_norm=mem_norm, xattn_wq=xattn_wq, xattn_wk=xattn_wk, xattn_wv=xattn_wv,
        xattn_q_norm=xattn_q_norm, xattn_k_norm=xattn_k_norm, xattn_wo=xattn_wo,
        ffn2_norm=ffn2_norm, ffn2_w_gate=ffn2_w_gate, ffn2_w_up=ffn2_w_up, ffn2_w_down=ffn2_w_down,
    )
    y_prompt, y_sample = x_prompt, x_sample
    for l in range(ffn1_norm.shape[0]):
        W = _prepare({name: a[l] for name, a in stacked.items()})
        y_prompt = _layer(y_prompt, mem_prompt, W)
        y_sample = _layer(y_sample, mem_sample, W)
    return (y_prompt, y_sample)
```

```python
import functools

import jax
import jax.numpy as jnp
from jax import lax
from jax.experimental import pallas as pl
from jax.experimental.pallas import tpu as pltpu

F32 = jnp.float32
BF16 = jnp.bfloat16

D_MODEL = 2048
D_FF = 5504
MEM_HEADS = 4
MEM_DH = D_MODEL // MEM_HEADS
GLA_HEADS = 4
GLA_DK = 128
GLA_DV = 256
GLA_GATE_RANK = 16
GLA_GATE_TEMP = 16.0
GLA_CHUNK = 64
MLA_HEADS = 8
MLA_Q_RANK = 512
MLA_KV_RANK = 256
MLA_NOPE = 128
MLA_ROPE = 64
MLA_DV = 128
ROPE_THETA = 10000.0
EPS = 1e-6
LOG2_E = 1.4426950408889634

GLA_QK = GLA_HEADS * GLA_DK
GLA_V = GLA_HEADS * GLA_DV
MLA_V = MLA_HEADS * MLA_DV
MLA_QK_HEAD = MLA_NOPE + MLA_ROPE

LANE = 128
MXU_DIM = 256
VMEM_LIMIT = 60 * 1024 * 1024

MLA_HEAD_PAD = MXU_DIM
GLA_BLOCK = MXU_DIM
D_FF_TILE = 512
FFN_ROWS = 1024
PROJ_ROWS = 512
MEM_ROWS = 256
PROJ_COLS = 1024
FLASH_Q = 1024
FLASH_K = 1024
Z_GLA = 2 * GLA_QK + 2 * GLA_V
Z_CQ = Z_GLA
Z_CKV = Z_CQ + MLA_Q_RANK
Z_KPE = Z_CKV + MLA_KV_RANK
Z_PAD = -(-(Z_KPE + MLA_ROPE + LANE) // PROJ_COLS) * PROJ_COLS
Z_GATES = Z_PAD - LANE


def _params(*sem):
    return pltpu.CompilerParams(dimension_semantics=sem, vmem_limit_bytes=VMEM_LIMIT)


def _resident(shape):
    return pl.BlockSpec(shape, lambda *_: (0,) * len(shape), pipeline_mode=pl.Buffered(1))


def _rms(x, g):
    return x * lax.rsqrt(jnp.mean(x * x, axis=-1, keepdims=True) + EPS) * g


def _dot(a, b):
    return jnp.dot(a, b, preferred_element_type=F32)


def _dot_nt(a, b):
    return lax.dot_general(a, b, (((1,), (1,)), ((), ())), preferred_element_type=F32)


def _dot_tn(a, b):
    return lax.dot_general(a, b, (((0,), (0,)), ((), ())), preferred_element_type=F32)


def _ffn_kernel(x_ref, g_ref, wg_ref, wu_ref, wd_ref, o_ref, xn_ref, *, overlap):
    j = pl.program_id(1)
    last = pl.num_programs(1) - 1

    @pl.when(j == 0)
    def _():
        x = x_ref[...]
        xn_ref[...] = _rms(x, g_ref[...]).astype(BF16)
        o_ref[...] = x

    xn = xn_ref[...]
    gate = _dot(xn, wg_ref[...])
    up = _dot(xn, wu_ref[...])
    hid = gate * jax.nn.sigmoid(gate) * up
    if overlap:
        col = lax.broadcasted_iota(jnp.int32, (1, hid.shape[1]), 1)
        hid = jnp.where((col >= overlap) | (j != last), hid, 0.0)
    o_ref[...] += _dot(hid.astype(BF16), wd_ref[...])


def _ffn(x, g, wg, wu, wd):
    t, d = x.shape
    f = wg.shape[1]
    tm = min(FFN_ROWS, t)
    tf = D_FF_TILE
    nj = -(-f // tf)
    assert f % LANE == 0 and tf % LANE == 0
    start = lambda j: jnp.minimum(j * (tf // LANE), (f - tf) // LANE) * LANE
    return pl.pallas_call(
        functools.partial(_ffn_kernel, overlap=nj * tf - f),
        grid=(t // tm, nj),
        in_specs=[
            pl.BlockSpec((tm, d), lambda i, j: (i, 0)),
            pl.BlockSpec((1, d), lambda i, j: (0, 0)),
            pl.BlockSpec((pl.Element(d), pl.Element(tf)), lambda i, j: (0, start(j))),
            pl.BlockSpec((pl.Element(d), pl.Element(tf)), lambda i, j: (0, start(j))),
            pl.BlockSpec((pl.Element(tf), pl.Element(d)), lambda i, j: (start(j), 0)),
        ],
        out_specs=pl.BlockSpec((tm, d), lambda i, j: (i, 0)),
        out_shape=jax.ShapeDtypeStruct((t, d), F32),
        scratch_shapes=[pltpu.VMEM((tm, d), BF16)],
        compiler_params=_params("parallel", "arbitrary"),
        name="ffn",
    )(x, g, wg, wu, wd)


def _log_sigmoid(x):
    return jnp.minimum(x, 0.0) - jnp.log(1.0 + jnp.exp(-jnp.abs(x)))


def _mixin_kernel(x_ref, g_ref, w_ref, w2f_ref, bf_ref, w2b_ref, bb_ref,
                  gq_ref, wuq_ref, gkv_ref, wukv_ref, gqq_ref, gqk_ref, cos_ref, sin_ref,
                  z_ref, laf_ref, lab_ref, q_ref, k_ref, v_ref):
    xn = _rms(x_ref[...], g_ref[...]).astype(BF16)
    tail = _dot(xn, w_ref[:, Z_GLA:])
    c = tail[:, Z_GATES - Z_GLA:].astype(BF16)
    laf_ref[...] = _log_sigmoid(_dot(c, w2f_ref[...]) + bf_ref[...]) * (1.0 / GLA_GATE_TEMP)
    lab_ref[...] = _log_sigmoid(_dot(c, w2b_ref[...]) + bb_ref[...]) * (1.0 / GLA_GATE_TEMP)
    _mla_heads(tail[:, Z_CQ - Z_GLA:Z_CKV - Z_GLA], tail[:, Z_CKV - Z_GLA:Z_KPE - Z_GLA],
               tail[:, Z_KPE - Z_GLA:Z_KPE - Z_GLA + LANE],
               gq_ref, wuq_ref, gkv_ref, wukv_ref, gqq_ref, gqk_ref, cos_ref[...], sin_ref[...],
               q_ref, k_ref, v_ref)
    for n in range(Z_GLA // PROJ_COLS):
        sl = slice(n * PROJ_COLS, (n + 1) * PROJ_COLS)
        z_ref[:, sl] = _dot(xn, w_ref[:, sl]).astype(z_ref.dtype)


def _mixin(x, seq, g, w, w2f, bf, w2b, bb, gq, wuq, gkv, wukv, gqq, gqk, cos, sin):
    t, d = x.shape
    tm = min(PROJ_ROWS, seq)
    nb = seq // tm
    hp = MLA_HEAD_PAD
    rows = lambda width: pl.BlockSpec((tm, width), lambda i: (i, 0))
    heads = lambda width: pl.BlockSpec((None, MLA_HEADS, tm, width), lambda i: (i // nb, 0, i % nb, 0))
    table = pl.BlockSpec((tm, LANE), lambda i: (i % nb, 0))
    return pl.pallas_call(
        _mixin_kernel,
        grid=(t // tm,),
        in_specs=[rows(d), _resident((1, d)), _resident(w.shape),
                  _resident((LANE, GLA_QK)), _resident((1, GLA_QK)),
                  _resident((LANE, GLA_QK)), _resident((1, GLA_QK)),
                  _resident((1, MLA_Q_RANK)), _resident(wuq.shape),
                  _resident((1, MLA_KV_RANK)), _resident(wukv.shape),
                  _resident((1, hp)), _resident((1, hp)), table, table],
        out_specs=[rows(Z_GLA), rows(GLA_QK), rows(GLA_QK), heads(hp), heads(hp), heads(MLA_DV)],
        out_shape=[
            jax.ShapeDtypeStruct((t, Z_GLA), BF16),
            jax.ShapeDtypeStruct((t, GLA_QK), F32),
            jax.ShapeDtypeStruct((t, GLA_QK), F32),
            jax.ShapeDtypeStruct((t // seq, MLA_HEADS, seq, hp), BF16),
            jax.ShapeDtypeStruct((t // seq, MLA_HEADS, seq, hp), BF16),
            jax.ShapeDtypeStruct((t // seq, MLA_HEADS, seq, MLA_DV), BF16),
        ],
        compiler_params=_params("parallel"),
        name="mixin",
    )(x, g, w, w2f, bf, w2b, bb, gq, wuq, gkv, wukv, gqq, gqk, cos, sin)


def _gla_kernel(q_ref, k_ref, v_ref, r_ref, laf_ref, lab_ref, gn_ref, o_ref,
                oacc_ref, sf_ref, sb_ref, *, seq):
    c = GLA_CHUNK
    blk = GLA_BLOCK
    cpb = blk // c
    n_blocks = seq // blk
    row = lax.broadcasted_iota(jnp.int32, (blk, blk), 0)
    col = lax.broadcasted_iota(jnp.int32, (blk, blk), 1)
    same_chunk = (row // c) == (col // c)
    lower = same_chunk & (col <= row)
    upper = same_chunk & (col >= row)
    row_in_chunk = lax.broadcasted_iota(jnp.int32, (blk, GLA_DK), 0) % c
    q_scale = GLA_DK ** -0.5

    sf_ref[...] = jnp.zeros_like(sf_ref)
    sb_ref[...] = jnp.zeros_like(sb_ref)

    def chunk_cumsum(x, forward):
        shift = 1
        while shift < c:
            if forward:
                x = x + jnp.where(row_in_chunk >= shift, pltpu.roll(x, shift, 0), 0.0)
            else:
                x = x + jnp.where(row_in_chunk < c - shift, pltpu.roll(x, blk - shift, 0), 0.0)
            shift *= 2
        return x

    def direction(rows, la_ref, mask, forward, st_ref):
        b = chunk_cumsum(la_ref[rows, :], forward)
        last = c - 1 if forward else 0
        b_end = [b[i * c + last:i * c + last + 1, :] for i in range(cpb)]
        bl = jnp.concatenate([jnp.broadcast_to(r, (c, GLA_DK)) for r in b_end], axis=0)
        b_t = [b[i * LANE:(i + 1) * LANE, :].T for i in range(blk // LANE)]
        q = q_ref[rows, :].astype(F32) * q_scale
        k = k_ref[rows, :].astype(F32)
        qt = (q * jnp.exp(b)).astype(BF16)
        kt = (k * jnp.exp(-b)).astype(BF16)
        ks = (k * jnp.exp(bl - b)).astype(BF16)
        v = v_ref[rows, :]
        a = jnp.where(mask, _dot_nt(qt, kt), 0.0).astype(BF16)
        o_intra = _dot(a, v)
        st = st_ref[...]
        outs = [None] * cpb
        for i in (range(cpb) if forward else reversed(range(cpb))):
            sl = slice(i * c, (i + 1) * c)
            end = (i * c + last) % LANE
            outs[i] = o_intra[sl] + _dot(qt[sl], st.astype(BF16))
            st = st * jnp.exp(b_t[i * c // LANE][:, end:end + 1]) + _dot_tn(ks[sl], v[sl])
        st_ref[...] = st
        return jnp.concatenate(outs, axis=0)

    def finish(rows, o):
        o = o * lax.rsqrt(jnp.mean(o * o, axis=-1, keepdims=True) + EPS) * gn_ref[...]
        r = r_ref[rows, :].astype(F32)
        o_ref[rows, :] = (o * (r * jax.nn.sigmoid(r))).astype(o_ref.dtype)

    def rows_of(n):
        return pl.ds(pl.multiple_of(n * blk, blk), blk)

    def first_half(n, carry):
        rf, rb = rows_of(n), rows_of(n_blocks - 1 - n)
        oacc_ref[rf, :] = direction(rf, laf_ref, lower, True, sf_ref)
        oacc_ref[rb, :] = direction(rb, lab_ref, upper, False, sb_ref)
        return carry

    def second_half(n, carry):
        rf, rb = rows_of(n), rows_of(n_blocks - 1 - n)
        finish(rf, oacc_ref[rf, :] + direction(rf, laf_ref, lower, True, sf_ref))
        finish(rb, oacc_ref[rb, :] + direction(rb, lab_ref, upper, False, sb_ref))
        return carry

    lax.fori_loop(0, n_blocks // 2, first_half, 0, unroll=4)
    lax.fori_loop(n_blocks // 2, n_blocks, second_half, 0, unroll=2)


def _gla(z, laf, lab, gn):
    b, s, _ = z.shape
    assert s % (2 * GLA_BLOCK) == 0
    k_blk = GLA_QK // GLA_DK
    v_blk = 2 * GLA_QK // GLA_DV
    r_blk = (2 * GLA_QK + GLA_V) // GLA_DV
    return pl.pallas_call(
        functools.partial(_gla_kernel, seq=s),
        grid=(b, GLA_HEADS),
        in_specs=[
            pl.BlockSpec((None, s, GLA_DK), lambda i, h: (i, 0, h)),
            pl.BlockSpec((None, s, GLA_DK), lambda i, h: (i, 0, k_blk + h)),
            pl.BlockSpec((None, s, GLA_DV), lambda i, h: (i, 0, v_blk + h)),
            pl.BlockSpec((None, s, GLA_DV), lambda i, h: (i, 0, r_blk + h)),
            pl.BlockSpec((None, s, GLA_DK), lambda i, h: (i, 0, h)),
            pl.BlockSpec((None, s, GLA_DK), lambda i, h: (i, 0, h)),
            pl.BlockSpec((1, GLA_DV), lambda i, h: (0, h)),
        ],
        out_specs=pl.BlockSpec((None, s, GLA_DV), lambda i, h: (i, 0, h)),
        out_shape=jax.ShapeDtypeStruct((b, s, GLA_V), BF16),
        scratch_shapes=[
            pltpu.VMEM((s, GLA_DV), F32),
            pltpu.VMEM((GLA_DK, GLA_DV), F32),
            pltpu.VMEM((GLA_DK, GLA_DV), F32),
        ],
        compiler_params=_params("parallel", "parallel"),
        name="gla",
    )(z, z, z, z, laf, lab, gn)


def _rope(x, cos, sin):
    half = MLA_ROPE // 2
    return x * cos - pltpu.roll(x, LANE - half, 1) * sin + pltpu.roll(x, half, 1) * sin


def _mla_heads(cq, ckv, kpe, gq_ref, wuq_ref, gkv_ref, wukv_ref, gqq_ref, gqk_ref, cos, sin,
               q_ref, k_ref, v_ref):
    inv_w = 1.0 / MLA_QK_HEAD
    scale = MLA_QK_HEAD ** -0.5 * LOG2_E
    hp = MLA_HEAD_PAD

    qn = _rms(cq, gq_ref[...]).astype(BF16)
    qf = _dot(qn, wuq_ref[...])
    gqq = gqq_ref[...] * scale
    for h in range(MLA_HEADS):
        qh = qf[:, h * hp:(h + 1) * hp]
        r = lax.rsqrt(jnp.sum(qh * qh, axis=-1, keepdims=True) * inv_w + EPS)
        q_ref[h, :, :MLA_NOPE] = (qh[:, :MLA_NOPE] * gqq[:, :MLA_NOPE] * r).astype(BF16)
        q_ref[h, :, MLA_NOPE:] = (_rope(qh[:, MLA_NOPE:] * gqq[:, MLA_NOPE:], cos, sin) * r).astype(BF16)

    kvn = _rms(ckv, gkv_ref[...]).astype(BF16)
    kvf = _dot(kvn, wukv_ref[...])
    kpe_ss = jnp.sum(kpe * kpe, axis=-1, keepdims=True)
    gqk = gqk_ref[...]
    k_rope = _rope(kpe * gqk[:, MLA_NOPE:], cos, sin)
    hw = MLA_NOPE + MLA_DV
    for h in range(MLA_HEADS):
        kn = kvf[:, h * hw:h * hw + MLA_NOPE]
        r = lax.rsqrt((jnp.sum(kn * kn, axis=-1, keepdims=True) + kpe_ss) * inv_w + EPS)
        k_ref[h, :, :MLA_NOPE] = (kn * gqk[:, :MLA_NOPE] * r).astype(BF16)
        k_ref[h, :, MLA_NOPE:] = (k_rope * r).astype(BF16)
        v_ref[h] = kvf[:, h * hw + MLA_NOPE:(h + 1) * hw].astype(BF16)


def _flash_kernel(q_ref, k_ref, v_ref, o_ref, *, seq, tk):
    q = q_ref[...]
    m = l = acc = None
    for n in range(seq // tk):
        s = _dot_nt(q, k_ref[pl.ds(n * tk, tk), :])
        v = v_ref[pl.ds(n * tk, tk), :]
        s_max = jnp.max(s, axis=-1, keepdims=True)
        if n == 0:
            m = s_max
            p = jnp.exp2(s - m)
            l = jnp.sum(p, axis=-1, keepdims=True)
            acc = _dot(p.astype(BF16), v)
        else:
            m_new = jnp.maximum(m, s_max)
            alpha = jnp.exp2(m - m_new)
            p = jnp.exp2(s - m_new)
            l = alpha * l + jnp.sum(p, axis=-1, keepdims=True)
            acc = alpha * acc + _dot(p.astype(BF16), v)
            m = m_new
    o_ref[...] = (acc / l).astype(o_ref.dtype)


def _flash(q, k, v):
    b, h, s, hp = q.shape
    tq = min(FLASH_Q, s)
    tk = min(FLASH_K, s)
    return pl.pallas_call(
        functools.partial(_flash_kernel, seq=s, tk=tk),
        grid=(b, h, s // tq),
        in_specs=[
            pl.BlockSpec((None, None, tq, hp), lambda i, j, n: (i, j, n, 0)),
            pl.BlockSpec((None, None, s, hp), lambda i, j, n: (i, j, 0, 0)),
            pl.BlockSpec((None, None, s, MLA_DV), lambda i, j, n: (i, j, 0, 0)),
        ],
        out_specs=pl.BlockSpec((None, tq, MLA_DV), lambda i, j, n: (i, n, j)),
        out_shape=jax.ShapeDtypeStruct((b, s, h * MLA_DV), BF16),
        compiler_params=_params("parallel", "parallel", "arbitrary"),
        name="mla_flash",
    )(q, k, v)


def _post1_kernel(a_ref, b_ref, res_ref, wa_ref, wb_ref, g_ref, wq_ref, hg_ref, h_ref, q_ref):
    h = res_ref[...] + _dot(a_ref[...], wa_ref[...]) + _dot(b_ref[...], wb_ref[...])
    h_ref[...] = h
    xn = _rms(h, g_ref[...]).astype(BF16)
    hg = hg_ref[...] * (MEM_DH ** -0.5 * LOG2_E)
    for n in range(q_ref.shape[1] // PROJ_COLS):
        sl = slice(n * PROJ_COLS, (n + 1) * PROJ_COLS)
        acc = _dot(xn, wq_ref[:, sl])
        for h2 in range(PROJ_COLS // MEM_DH):
            hs = slice(h2 * MEM_DH, (h2 + 1) * MEM_DH)
            q_ref[:, n * PROJ_COLS + h2 * MEM_DH:n * PROJ_COLS + (h2 + 1) * MEM_DH] = (
                _rms(acc[:, hs], hg).astype(q_ref.dtype))


def _post1(a, b, res, wa, wb, g, wq, hg):
    t, d = res.shape
    tm = min(PROJ_ROWS, t)
    rows = lambda width: pl.BlockSpec((tm, width), lambda i: (i, 0))
    return pl.pallas_call(
        _post1_kernel,
        grid=(t // tm,),
        in_specs=[rows(a.shape[1]), rows(b.shape[1]), rows(d),
                  _resident(wa.shape), _resident(wb.shape), _resident((1, d)),
                  _resident(wq.shape), _resident((1, MEM_DH))],
        out_specs=[rows(d), rows(d)],
        out_shape=[jax.ShapeDtypeStruct((t, d), F32), jax.ShapeDtypeStruct((t, d), BF16)],
        compiler_params=_params("parallel"),
        name="post1",
    )(a, b, res, wa, wb, g, wq, hg)


def _post2_kernel(q_ref, k_ref, v_ref, res_ref, wo_ref, o_ref, att_ref):
    for h in range(MEM_HEADS):
        sl = slice(h * MEM_DH, (h + 1) * MEM_DH)
        s = _dot_nt(q_ref[:, sl], k_ref[:, sl])
        p = jnp.exp2(s - jnp.max(s, axis=-1, keepdims=True))
        l = jnp.sum(p, axis=-1, keepdims=True)
        att_ref[:, sl] = (_dot(p.astype(BF16), v_ref[:, sl]) / l).astype(BF16)
    o_ref[...] = res_ref[...] + _dot(att_ref[...], wo_ref[...])


def _post2(q, k, v, res, wo):
    b, s, d = q.shape
    m = k.shape[1]
    tm = min(PROJ_ROWS, s)
    tok = pl.BlockSpec((None, tm, d), lambda i, j: (i, j, 0))
    mem = pl.BlockSpec((None, m, d), lambda i, j: (i, 0, 0))
    return pl.pallas_call(
        _post2_kernel,
        grid=(b, s // tm),
        in_specs=[tok, mem, mem, tok, _resident(wo.shape)],
        out_specs=tok,
        out_shape=jax.ShapeDtypeStruct((b, s, d), F32),
        scratch_shapes=[pltpu.VMEM((tm, d), BF16)],
        compiler_params=_params("parallel", "arbitrary"),
        name="post2",
    )(q, k, v, res, wo)


def _memkv_kernel(x_ref, g_ref, wk_ref, wv_ref, hg_ref, k_ref, v_ref):
    xn = _rms(x_ref[...], g_ref[...]).astype(BF16)
    hg = hg_ref[...]
    for n in range(k_ref.shape[1] // PROJ_COLS):
        sl = slice(n * PROJ_COLS, (n + 1) * PROJ_COLS)
        acc = _dot(xn, wk_ref[:, sl])
        for h in range(PROJ_COLS // MEM_DH):
            hs = slice(h * MEM_DH, (h + 1) * MEM_DH)
            k_ref[:, n * PROJ_COLS + h * MEM_DH:n * PROJ_COLS + (h + 1) * MEM_DH] = (
                _rms(acc[:, hs], hg).astype(k_ref.dtype))
        v_ref[:, sl] = _dot(xn, wv_ref[:, sl]).astype(v_ref.dtype)


def _memkv(x, g, wk, wv, hg):
    t, d = x.shape
    tm = min(MEM_ROWS, t)
    rows = pl.BlockSpec((tm, d), lambda i: (i, 0))
    return pl.pallas_call(
        _memkv_kernel,
        grid=(t // tm,),
        in_specs=[rows, _resident((1, d)), _resident(wk.shape), _resident(wv.shape), _resident((1, MEM_DH))],
        out_specs=[rows, rows],
        out_shape=[jax.ShapeDtypeStruct((t, d), BF16), jax.ShapeDtypeStruct((t, d), BF16)],
        compiler_params=_params("parallel"),
        name="mem_kv",
    )(x, g, wk, wv, hg)


def _rope_tables(seq):
    half = MLA_ROPE // 2
    inv = ROPE_THETA ** (-jnp.arange(half, dtype=F32) / half)
    ang = jnp.arange(seq, dtype=jnp.int32).astype(F32)[:, None] * inv[None, :]
    pad = jnp.zeros((seq, LANE - MLA_ROPE), F32)
    cos = jnp.concatenate([jnp.cos(ang), jnp.cos(ang), pad], axis=-1)
    sin = jnp.concatenate([jnp.sin(ang), jnp.sin(ang), pad], axis=-1)
    return cos, sin


def _pad_cols(w, n):
    return jnp.concatenate([w, jnp.zeros((w.shape[0], n - w.shape[1]), w.dtype)], axis=1)


def _prepare(P):
    row = lambda a: a.reshape(1, -1).astype(F32)
    W = {}
    for pre in ("ffn1", "ffn2"):
        W[pre] = (row(P[pre + "_norm"]),
                  P[pre + "_w_gate"].astype(BF16), P[pre + "_w_up"].astype(BF16),
                  (0.5 * P[pre + "_w_down"]).astype(BF16))
    w_in = P["w_in"].astype(BF16)
    g0, g1 = Z_GLA, Z_GLA + 2 * GLA_GATE_RANK
    rank = GLA_GATE_RANK
    W["mixin"] = (row(P["mix_norm"]),
                  jnp.concatenate([w_in[:, :g0], _pad_cols(w_in[:, g1:], Z_GATES - Z_GLA),
                                   _pad_cols(w_in[:, g0:g1], LANE)], axis=1),
                  jnp.pad(P["gla_wa2_fwd"].astype(BF16), ((0, LANE - rank), (0, 0))), row(P["gla_ba_fwd"]),
                  jnp.pad(P["gla_wa2_bwd"].astype(BF16), ((rank, LANE - 2 * rank), (0, 0))), row(P["gla_ba_bwd"]))
    W["gla_norm"] = row(P["gla_out_norm"])
    hp = MLA_HEAD_PAD
    wuq = P["mla_w_uq"].astype(BF16).reshape(MLA_Q_RANK, MLA_HEADS, MLA_QK_HEAD)
    wuq = jnp.pad(wuq, ((0, 0), (0, 0), (0, hp - MLA_QK_HEAD))).reshape(MLA_Q_RANK, MLA_HEADS * hp)
    W["mla"] = (row(P["mla_q_norm"]), wuq, row(P["mla_kv_norm"]), P["mla_w_ukv"].astype(BF16),
                _pad_cols(row(P["mla_qk_q_norm"]), hp), _pad_cols(row(P["mla_qk_k_norm"]), hp))
    w_out = P["w_out"].astype(BF16)
    W["post1"] = (w_out[:GLA_V], w_out[GLA_V:], row(P["xattn_norm"]), P["xattn_wq"].astype(BF16),
                  row(P["xattn_q_norm"]))
    W["memkv"] = (row(P["mem_norm"]), P["xattn_wk"].astype(BF16), P["xattn_wv"].astype(BF16),
                  row(P["xattn_k_norm"]))
    W["wo"] = P["xattn_wo"].astype(BF16)
    return W


def _layer(x, mem, W):
    b, s, d = x.shape
    m = mem.shape[1]
    t = b * s

    h = _ffn(x.reshape(t, d), *W["ffn1"])

    z, laf, lab, q, k, v = _mixin(h, s, *W["mixin"], *W["mla"], *_rope_tables(s))
    y_gla = _gla(z.reshape(b, s, Z_GLA), laf.reshape(b, s, GLA_QK), lab.reshape(b, s, GLA_QK), W["gla_norm"])
    y_mla = _flash(q, k, v)

    h, xq = _post1(y_gla.reshape(t, GLA_V), y_mla.reshape(t, MLA_V), h, *W["post1"])
    xk, xv = _memkv(mem.reshape(b * m, d), *W["memkv"])
    h = _post2(xq.reshape(b, s, d), xk.reshape(b, m, d), xv.reshape(b, m, d), h.reshape(b, s, d), W["wo"])

    h = _ffn(h.reshape(t, d), *W["ffn2"])
    return h.reshape(b, s, d)


def kernel(x_prompt, x_sample, mem_prompt, mem_sample, ffn1_norm, ffn1_w_gate, ffn1_w_up, ffn1_w_down, mix_norm, w_in, gla_wa2_fwd, gla_ba_fwd, gla_wa2_bwd, gla_ba_bwd, gla_out_norm, mla_q_norm, mla_w_uq, mla_kv_norm, mla_w_ukv, mla_qk_q_norm, mla_qk_k_norm, w_out, xattn_norm, mem_norm, xattn_wq, xattn_wk, xattn_wv, xattn_q_norm, xattn_k_norm, xattn_wo, ffn2_norm, ffn2_w_gate, ffn2_w_up, ffn2_w_down):
    stacked = dict(
        ffn1_norm=ffn1_norm, ffn1_w_gate=ffn1_w_gate, ffn1_w_up=ffn1_w_up, ffn1_w_down=ffn1_w_down,
        mix_norm=mix_norm, w_in=w_in,
        gla_wa2_fwd=gla_wa2_fwd, gla_ba_fwd=gla_ba_fwd, gla_wa2_bwd=gla_wa2_bwd, gla_ba_bwd=gla_ba_bwd,
        gla_out_norm=gla_out_norm,
        mla_q_norm=mla_q_norm, mla_w_uq=mla_w_uq, mla_kv_norm=mla_kv_norm, mla_w_ukv=mla_w_ukv,
        mla_qk_q_norm=mla_qk_q_norm, mla_qk_k_norm=mla_qk_k_norm,
        w_out=w_out,
        xattn_norm=xattn_norm, mem_norm=mem_norm, xattn_wq=xattn_wq, xattn_wk=xattn_wk, xattn_wv=xattn_wv,
        xattn_q_norm=xattn_q_norm, xattn_k_norm=xattn_k_norm, xattn_wo=xattn_wo,
        ffn2_norm=ffn2_norm, ffn2_w_gate=ffn2_w_gate, ffn2_w_up=ffn2_w_up, ffn2_w_down=ffn2_w_down,
    )
    y_prompt, y_sample = x_prompt, x_sample
    for l in range(ffn1_norm.shape[0]):
        W = _prepare({name: a[l] for name, a in stacked.items()})
        y_prompt = _layer(y_prompt, mem_prompt, W)
        y_sample = _layer(y_sample, mem_sample, W)
    return (y_prompt, y_sample)
```

```python
import functools

import jax
import jax.numpy as jnp
from jax import lax
from jax.experimental import pallas as pl
from jax.experimental.pallas import tpu as pltpu

F32 = jnp.float32
BF16 = jnp.bfloat16

D_MODEL = 2048
D_FF = 5504
MEM_HEADS = 4
MEM_DH = D_MODEL // MEM_HEADS
GLA_HEADS = 4
GLA_DK = 128
GLA_DV = 256
GLA_GATE_RANK = 16
GLA_GATE_TEMP = 16.0
GLA_CHUNK = 64
MLA_HEADS = 8
MLA_Q_RANK = 512
MLA_KV_RANK = 256
MLA_NOPE = 128
MLA_ROPE = 64
MLA_DV = 128
ROPE_THETA = 10000.0
EPS = 1e-6
LOG2_E = 1.4426950408889634

GLA_QK = GLA_HEADS * GLA_DK
GLA_V = GLA_HEADS * GLA_DV
MLA_V = MLA_HEADS * MLA_DV
MLA_QK_HEAD = MLA_NOPE + MLA_ROPE

LANE = 128
MXU_DIM = 256
VMEM_LIMIT = 60 * 1024 * 1024

MLA_HEAD_PAD = MXU_DIM
GLA_BLOCK = MXU_DIM
D_FF_TILE = 512
FFN_ROWS = 1024
PROJ_ROWS = 512
MEM_ROWS = 256
PROJ_COLS = 1024
FLASH_Q = 2048
FLASH_K = 1024
Z_GLA = 2 * GLA_QK + 2 * GLA_V
Z_CQ = Z_GLA
Z_CKV = Z_CQ + MLA_Q_RANK
Z_KPE = Z_CKV + MLA_KV_RANK
Z_PAD = -(-(Z_KPE + MLA_ROPE + LANE) // PROJ_COLS) * PROJ_COLS
Z_GATES = Z_PAD - LANE


def _params(*sem):
    return pltpu.CompilerParams(dimension_semantics=sem, vmem_limit_bytes=VMEM_LIMIT)


def _resident(shape):
    return pl.BlockSpec(shape, lambda *_: (0,) * len(shape), pipeline_mode=pl.Buffered(1))


def _rms(x, g):
    return x * lax.rsqrt(jnp.mean(x * x, axis=-1, keepdims=True) + EPS) * g


def _dot(a, b):
    return jnp.dot(a, b, preferred_element_type=F32)


def _dot_nt(a, b):
    return lax.dot_general(a, b, (((1,), (1,)), ((), ())), preferred_element_type=F32)


def _dot_tn(a, b):
    return lax.dot_general(a, b, (((0,), (0,)), ((), ())), preferred_element_type=F32)


def _ffn_kernel(x_ref, g_ref, wg_ref, wu_ref, wd_ref, o_ref, xn_ref, *, overlap):
    j = pl.program_id(1)
    last = pl.num_programs(1) - 1

    @pl.when(j == 0)
    def _():
        x = x_ref[...]
        xn_ref[...] = _rms(x, g_ref[...]).astype(BF16)
        o_ref[...] = x

    xn = xn_ref[...]
    gate = _dot(xn, wg_ref[...])
    up = _dot(xn, wu_ref[...])
    hid = gate * jax.nn.sigmoid(gate) * up
    if overlap:
        col = lax.broadcasted_iota(jnp.int32, (1, hid.shape[1]), 1)
        hid = jnp.where((col >= overlap) | (j != last), hid, 0.0)
    o_ref[...] += _dot(hid.astype(BF16), wd_ref[...])


def _ffn(x, g, wg, wu, wd):
    t, d = x.shape
    f = wg.shape[1]
    tm = min(FFN_ROWS, t)
    tf = D_FF_TILE
    nj = -(-f // tf)
    assert f % LANE == 0 and tf % LANE == 0
    start = lambda j: jnp.minimum(j * (tf // LANE), (f - tf) // LANE) * LANE
    return pl.pallas_call(
        functools.partial(_ffn_kernel, overlap=nj * tf - f),
        grid=(t // tm, nj),
        in_specs=[
            pl.BlockSpec((tm, d), lambda i, j: (i, 0)),
            pl.BlockSpec((1, d), lambda i, j: (0, 0)),
            pl.BlockSpec((pl.Element(d), pl.Element(tf)), lambda i, j: (0, start(j))),
            pl.BlockSpec((pl.Element(d), pl.Element(tf)), lambda i, j: (0, start(j))),
            pl.BlockSpec((pl.Element(tf), pl.Element(d)), lambda i, j: (start(j), 0)),
        ],
        out_specs=pl.BlockSpec((tm, d), lambda i, j: (i, 0)),
        out_shape=jax.ShapeDtypeStruct((t, d), F32),
        scratch_shapes=[pltpu.VMEM((tm, d), BF16)],
        compiler_params=_params("parallel", "arbitrary"),
        name="ffn",
    )(x, g, wg, wu, wd)


def _log_sigmoid(x):
    return jnp.minimum(x, 0.0) - jnp.log(1.0 + jnp.exp(-jnp.abs(x)))


def _mixin_kernel(x_ref, g_ref, w_ref, w2f_ref, bf_ref, w2b_ref, bb_ref,
                  gq_ref, wuq_ref, gkv_ref, wukv_ref, gqq_ref, gqk_ref, cos_ref, sin_ref,
                  z_ref, laf_ref, lab_ref, q_ref, k_ref, v_ref):
    xn = _rms(x_ref[...], g_ref[...]).astype(BF16)
    tail = _dot(xn, w_ref[:, Z_GLA:])
    c = tail[:, Z_GATES - Z_GLA:].astype(BF16)
    laf_ref[...] = _log_sigmoid(_dot(c, w2f_ref[...]) + bf_ref[...]) * (1.0 / GLA_GATE_TEMP)
    lab_ref[...] = _log_sigmoid(_dot(c, w2b_ref[...]) + bb_ref[...]) * (1.0 / GLA_GATE_TEMP)
    _mla_heads(tail[:, Z_CQ - Z_GLA:Z_CKV - Z_GLA], tail[:, Z_CKV - Z_GLA:Z_KPE - Z_GLA],
               tail[:, Z_KPE - Z_GLA:Z_KPE - Z_GLA + LANE],
               gq_ref, wuq_ref, gkv_ref, wukv_ref, gqq_ref, gqk_ref, cos_ref[...], sin_ref[...],
               q_ref, k_ref, v_ref)
    for n in range(Z_GLA // PROJ_COLS):
        sl = slice(n * PROJ_COLS, (n + 1) * PROJ_COLS)
        z_ref[:, sl] = _dot(xn, w_ref[:, sl]).astype(z_ref.dtype)


def _mixin(x, seq, g, w, w2f, bf, w2b, bb, gq, wuq, gkv, wukv, gqq, gqk, cos, sin):
    t, d = x.shape
    tm = min(PROJ_ROWS, seq)
    nb = seq // tm
    hp = MLA_HEAD_PAD
    rows = lambda width: pl.BlockSpec((tm, width), lambda i: (i, 0))
    heads = lambda width: pl.BlockSpec((None, MLA_HEADS, tm, width), lambda i: (i // nb, 0, i % nb, 0))
    table = pl.BlockSpec((tm, LANE), lambda i: (i % nb, 0))
    return pl.pallas_call(
        _mixin_kernel,
        grid=(t // tm,),
        in_specs=[rows(d), _resident((1, d)), _resident(w.shape),
                  _resident((LANE, GLA_QK)), _resident((1, GLA_QK)),
                  _resident((LANE, GLA_QK)), _resident((1, GLA_QK)),
                  _resident((1, MLA_Q_RANK)), _resident(wuq.shape),
                  _resident((1, MLA_KV_RANK)), _resident(wukv.shape),
                  _resident((1, hp)), _resident((1, hp)), table, table],
        out_specs=[rows(Z_GLA), rows(GLA_QK), rows(GLA_QK), heads(hp), heads(hp), heads(MLA_DV)],
        out_shape=[
            jax.ShapeDtypeStruct((t, Z_GLA), BF16),
            jax.ShapeDtypeStruct((t, GLA_QK), F32),
            jax.ShapeDtypeStruct((t, GLA_QK), F32),
            jax.ShapeDtypeStruct((t // seq, MLA_HEADS, seq, hp), BF16),
            jax.ShapeDtypeStruct((t // seq, MLA_HEADS, seq, hp), BF16),
            jax.ShapeDtypeStruct((t // seq, MLA_HEADS, seq, MLA_DV), BF16),
        ],
        compiler_params=_params("parallel"),
        name="mixin",
    )(x, g, w, w2f, bf, w2b, bb, gq, wuq, gkv, wukv, gqq, gqk, cos, sin)


def _gla_kernel(q_ref, k_ref, v_ref, r_ref, laf_ref, lab_ref, gn_ref, o_ref,
                oacc_ref, sf_ref, sb_ref, *, seq):
    c = GLA_CHUNK
    blk = GLA_BLOCK
    cpb = blk // c
    n_blocks = seq // blk
    row = lax.broadcasted_iota(jnp.int32, (blk, blk), 0)
    col = lax.broadcasted_iota(jnp.int32, (blk, blk), 1)
    same_chunk = (row // c) == (col // c)
    lower = same_chunk & (col <= row)
    upper = same_chunk & (col >= row)
    row_in_chunk = lax.broadcasted_iota(jnp.int32, (blk, GLA_DK), 0) % c
    q_scale = GLA_DK ** -0.5

    sf_ref[...] = jnp.zeros_like(sf_ref)
    sb_ref[...] = jnp.zeros_like(sb_ref)

    def chunk_cumsum(x, forward):
        shift = 1
        while shift < c:
            if forward:
                x = x + jnp.where(row_in_chunk >= shift, pltpu.roll(x, shift, 0), 0.0)
            else:
                x = x + jnp.where(row_in_chunk < c - shift, pltpu.roll(x, blk - shift, 0), 0.0)
            shift *= 2
        return x

    def direction(rows, la_ref, mask, forward, st_ref):
        b = chunk_cumsum(la_ref[rows, :], forward)
        last = c - 1 if forward else 0
        b_end = [b[i * c + last:i * c + last + 1, :] for i in range(cpb)]
        bl = jnp.concatenate([jnp.broadcast_to(r, (c, GLA_DK)) for r in b_end], axis=0)
        b_t = [b[i * LANE:(i + 1) * LANE, :].T for i in range(blk // LANE)]
        q = q_ref[rows, :].astype(F32) * q_scale
        k = k_ref[rows, :].astype(F32)
        qt = (q * jnp.exp(b)).astype(BF16)
        kt = (k * jnp.exp(-b)).astype(BF16)
        ks = (k * jnp.exp(bl - b)).astype(BF16)
        v = v_ref[rows, :]
        a = jnp.where(mask, _dot_nt(qt, kt), 0.0).astype(BF16)
        o_intra = _dot(a, v)
        st = st_ref[...]
        outs = [None] * cpb
        for i in (range(cpb) if forward else reversed(range(cpb))):
            sl = slice(i * c, (i + 1) * c)
            end = (i * c + last) % LANE
            outs[i] = o_intra[sl] + _dot(qt[sl], st.astype(BF16))
            st = st * jnp.exp(b_t[i * c // LANE][:, end:end + 1]) + _dot_tn(ks[sl], v[sl])
        st_ref[...] = st
        return jnp.concatenate(outs, axis=0)

    def finish(rows, o):
        o = o * lax.rsqrt(jnp.mean(o * o, axis=-1, keepdims=True) + EPS) * gn_ref[...]
        r = r_ref[rows, :].astype(F32)
        o_ref[rows, :] = (o * (r * jax.nn.sigmoid(r))).astype(o_ref.dtype)

    def rows_of(n):
        return pl.ds(pl.multiple_of(n * blk, blk), blk)

    def first_half(n, carry):
        rf, rb = rows_of(n), rows_of(n_blocks - 1 - n)
        oacc_ref[rf, :] = direction(rf, laf_ref, lower, True, sf_ref)
        oacc_ref[rb, :] = direction(rb, lab_ref, upper, False, sb_ref)
        return carry

    def second_half(n, carry):
        rf, rb = rows_of(n), rows_of(n_blocks - 1 - n)
        finish(rf, oacc_ref[rf, :] + direction(rf, laf_ref, lower, True, sf_ref))
        finish(rb, oacc_ref[rb, :] + direction(rb, lab_ref, upper, False, sb_ref))
        return carry

    lax.fori_loop(0, n_blocks // 2, first_half, 0, unroll=4)
    lax.fori_loop(n_blocks // 2, n_blocks, second_half, 0, unroll=2)


def _gla(z, laf, lab, gn):
    b, s, _ = z.shape
    assert s % (2 * GLA_BLOCK) == 0
    k_blk = GLA_QK // GLA_DK
    v_blk = 2 * GLA_QK // GLA_DV
    r_blk = (2 * GLA_QK + GLA_V) // GLA_DV
    return pl.pallas_call(
        functools.partial(_gla_kernel, seq=s),
        grid=(b, GLA_HEADS),
        in_specs=[
            pl.BlockSpec((None, s, GLA_DK), lambda i, h: (i, 0, h)),
            pl.BlockSpec((None, s, GLA_DK), lambda i, h: (i, 0, k_blk + h)),
            pl.BlockSpec((None, s, GLA_DV), lambda i, h: (i, 0, v_blk + h)),
            pl.BlockSpec((None, s, GLA_DV), lambda i, h: (i, 0, r_blk + h)),
            pl.BlockSpec((None, s, GLA_DK), lambda i, h: (i, 0, h)),
            pl.BlockSpec((None, s, GLA_DK), lambda i, h: (i, 0, h)),
            pl.BlockSpec((1, GLA_DV), lambda i, h: (0, h)),
        ],
        out_specs=pl.BlockSpec((None, s, GLA_DV), lambda i, h: (i, 0, h)),
        out_shape=jax.ShapeDtypeStruct((b, s, GLA_V), BF16),
        scratch_shapes=[
            pltpu.VMEM((s, GLA_DV), F32),
            pltpu.VMEM((GLA_DK, GLA_DV), F32),
            pltpu.VMEM((GLA_DK, GLA_DV), F32),
        ],
        compiler_params=_params("parallel", "parallel"),
        name="gla",
    )(z, z, z, z, laf, lab, gn)


def _rope(x, cos, sin):
    half = MLA_ROPE // 2
    return x * cos - pltpu.roll(x, LANE - half, 1) * sin + pltpu.roll(x, half, 1) * sin


def _mla_heads(cq, ckv, kpe, gq_ref, wuq_ref, gkv_ref, wukv_ref, gqq_ref, gqk_ref, cos, sin,
               q_ref, k_ref, v_ref):
    inv_w = 1.0 / MLA_QK_HEAD
    scale = MLA_QK_HEAD ** -0.5 * LOG2_E
    hp = MLA_HEAD_PAD

    qn = _rms(cq, gq_ref[...]).astype(BF16)
    qf = _dot(qn, wuq_ref[...])
    gqq = gqq_ref[...] * scale
    for h in range(MLA_HEADS):
        qh = qf[:, h * hp:(h + 1) * hp]
        r = lax.rsqrt(jnp.sum(qh * qh, axis=-1, keepdims=True) * inv_w + EPS)
        q_ref[h, :, :MLA_NOPE] = (qh[:, :MLA_NOPE] * gqq[:, :MLA_NOPE] * r).astype(BF16)
        q_ref[h, :, MLA_NOPE:] = (_rope(qh[:, MLA_NOPE:] * gqq[:, MLA_NOPE:], cos, sin) * r).astype(BF16)

    kvn = _rms(ckv, gkv_ref[...]).astype(BF16)
    kvf = _dot(kvn, wukv_ref[...])
    kpe_ss = jnp.sum(kpe * kpe, axis=-1, keepdims=True)
    gqk = gqk_ref[...]
    k_rope = _rope(kpe * gqk[:, MLA_NOPE:], cos, sin)
    hw = MLA_NOPE + MLA_DV
    for h in range(MLA_HEADS):
        kn = kvf[:, h * hw:h * hw + MLA_NOPE]
        r = lax.rsqrt((jnp.sum(kn * kn, axis=-1, keepdims=True) + kpe_ss) * inv_w + EPS)
        k_ref[h, :, :MLA_NOPE] = (kn * gqk[:, :MLA_NOPE] * r).astype(BF16)
        k_ref[h, :, MLA_NOPE:] = (k_rope * r).astype(BF16)
        v_ref[h] = kvf[:, h * hw + MLA_NOPE:(h + 1) * hw].astype(BF16)


def _flash_kernel(q_ref, k_ref, v_ref, o_ref, *, seq, tk):
    q = q_ref[...]
    m = l = acc = None
    for n in range(seq // tk):
        s = _dot_nt(q, k_ref[pl.ds(n * tk, tk), :])
        v = v_ref[pl.ds(n * tk, tk), :]
        s_max = jnp.max(s, axis=-1, keepdims=True)
        if n == 0:
            m = s_max
            p = jnp.exp2(s - m)
            l = jnp.sum(p, axis=-1, keepdims=True)
            acc = _dot(p.astype(BF16), v)
        else:
            m_new = jnp.maximum(m, s_max)
            alpha = jnp.exp2(m - m_new)
            p = jnp.exp2(s - m_new)
            l = alpha * l + jnp.sum(p, axis=-1, keepdims=True)
            acc = alpha * acc + _dot(p.astype(BF16), v)
            m = m_new
    o_ref[...] = (acc / l).astype(o_ref.dtype)


def _flash(q, k, v):
    b, h, s, hp = q.shape
    tq = min(FLASH_Q, s)
    tk = min(FLASH_K, s)
    return pl.pallas_call(
        functools.partial(_flash_kernel, seq=s, tk=tk),
        grid=(b, h, s // tq),
        in_specs=[
            pl.BlockSpec((None, None, tq, hp), lambda i, j, n: (i, j, n, 0)),
            pl.BlockSpec((None, None, s, hp), lambda i, j, n: (i, j, 0, 0)),
            pl.BlockSpec((None, None, s, MLA_DV), lambda i, j, n: (i, j, 0, 0)),
        ],
        out_specs=pl.BlockSpec((None, tq, MLA_DV), lambda i, j, n: (i, n, j)),
        out_shape=jax.ShapeDtypeStruct((b, s, h * MLA_DV), BF16),
        compiler_params=_params("parallel", "parallel", "arbitrary"),
        name="mla_flash",
    )(q, k, v)


def _post1_kernel(a_ref, b_ref, res_ref, w_ref, g_ref, wq_ref, hg_ref, h_ref, q_ref):
    ka = a_ref.shape[1]
    h = res_ref[...] + _dot(a_ref[...], w_ref[:ka, :]) + _dot(b_ref[...], w_ref[ka:, :])
    h_ref[...] = h
    xn = _rms(h, g_ref[...]).astype(BF16)
    hg = hg_ref[...] * (MEM_DH ** -0.5 * LOG2_E)
    for n in range(q_ref.shape[1] // PROJ_COLS):
        sl = slice(n * PROJ_COLS, (n + 1) * PROJ_COLS)
        acc = _dot(xn, wq_ref[:, sl])
        for h2 in range(PROJ_COLS // MEM_DH):
            hs = slice(h2 * MEM_DH, (h2 + 1) * MEM_DH)
            q_ref[:, n * PROJ_COLS + h2 * MEM_DH:n * PROJ_COLS + (h2 + 1) * MEM_DH] = (
                _rms(acc[:, hs], hg).astype(q_ref.dtype))


def _post1(a, b, res, w, g, wq, hg):
    t, d = res.shape
    tm = min(PROJ_ROWS, t)
    rows = lambda width: pl.BlockSpec((tm, width), lambda i: (i, 0))
    return pl.pallas_call(
        _post1_kernel,
        grid=(t // tm,),
        in_specs=[rows(a.shape[1]), rows(b.shape[1]), rows(d),
                  _resident(w.shape), _resident((1, d)), _resident(wq.shape), _resident((1, MEM_DH))],
        out_specs=[rows(d), rows(d)],
        out_shape=[jax.ShapeDtypeStruct((t, d), F32), jax.ShapeDtypeStruct((t, d), BF16)],
        compiler_params=_params("parallel"),
        name="post1",
    )(a, b, res, w, g, wq, hg)


def _post2_kernel(q_ref, k_ref, v_ref, res_ref, wo_ref, o_ref, att_ref):
    for h in range(MEM_HEADS):
        sl = slice(h * MEM_DH, (h + 1) * MEM_DH)
        s = _dot_nt(q_ref[:, sl], k_ref[:, sl])
        p = jnp.exp2(s - jnp.max(s, axis=-1, keepdims=True))
        l = jnp.sum(p, axis=-1, keepdims=True)
        att_ref[:, sl] = (_dot(p.astype(BF16), v_ref[:, sl]) / l).astype(BF16)
    o_ref[...] = res_ref[...] + _dot(att_ref[...], wo_ref[...])


def _post2(q, k, v, res, wo):
    b, s, d = q.shape
    m = k.shape[1]
    tm = min(PROJ_ROWS, s)
    tok = pl.BlockSpec((None, tm, d), lambda i, j: (i, j, 0))
    mem = pl.BlockSpec((None, m, d), lambda i, j: (i, 0, 0))
    return pl.pallas_call(
        _post2_kernel,
        grid=(b, s // tm),
        in_specs=[tok, mem, mem, tok, _resident(wo.shape)],
        out_specs=tok,
        out_shape=jax.ShapeDtypeStruct((b, s, d), F32),
        scratch_shapes=[pltpu.VMEM((tm, d), BF16)],
        compiler_params=_params("parallel", "arbitrary"),
        name="post2",
    )(q, k, v, res, wo)


def _memkv_kernel(x_ref, g_ref, wk_ref, wv_ref, hg_ref, k_ref, v_ref):
    xn = _rms(x_ref[...], g_ref[...]).astype(BF16)
    hg = hg_ref[...]
    for n in range(k_ref.shape[1] // PROJ_COLS):
        sl = slice(n * PROJ_COLS, (n + 1) * PROJ_COLS)
        acc = _dot(xn, wk_ref[:, sl])
        for h in range(PROJ_COLS // MEM_DH):
            hs = slice(h * MEM_DH, (h + 1) * MEM_DH)
            k_ref[:, n * PROJ_COLS + h * MEM_DH:n * PROJ_COLS + (h + 1) * MEM_DH] = (
                _rms(acc[:, hs], hg).astype(k_ref.dtype))
        v_ref[:, sl] = _dot(xn, wv_ref[:, sl]).astype(v_ref.dtype)


def _memkv(x, g, wk, wv, hg):
    t, d = x.shape
    tm = min(MEM_ROWS, t)
    rows = pl.BlockSpec((tm, d), lambda i: (i, 0))
    return pl.pallas_call(
        _memkv_kernel,
        grid=(t // tm,),
        in_specs=[rows, _resident((1, d)), _resident(wk.shape), _resident(wv.shape), _resident((1, MEM_DH))],
        out_specs=[rows, rows],
        out_shape=[jax.ShapeDtypeStruct((t, d), BF16), jax.ShapeDtypeStruct((t, d), BF16)],
        compiler_params=_params("parallel"),
        name="mem_kv",
    )(x, g, wk, wv, hg)


def _rope_tables(seq):
    half = MLA_ROPE // 2
    inv = ROPE_THETA ** (-jnp.arange(half, dtype=F32) / half)
    ang = jnp.arange(seq, dtype=jnp.int32).astype(F32)[:, None] * inv[None, :]
    pad = jnp.zeros((seq, LANE - MLA_ROPE), F32)
    cos = jnp.concatenate([jnp.cos(ang), jnp.cos(ang), pad], axis=-1)
    sin = jnp.concatenate([jnp.sin(ang), jnp.sin(ang), pad], axis=-1)
    return cos, sin


def _pad_cols(w, n):
    return jnp.concatenate([w, jnp.zeros((w.shape[0], n - w.shape[1]), w.dtype)], axis=1)


def _prepare(P):
    row = lambda a: a.reshape(1, -1).astype(F32)
    W = {}
    for pre in ("ffn1", "ffn2"):
        W[pre] = (row(P[pre + "_norm"]),
                  P[pre + "_w_gate"].astype(BF16), P[pre + "_w_up"].astype(BF16),
                  (0.5 * P[pre + "_w_down"]).astype(BF16))
    w_in = P["w_in"]
    g0, g1 = Z_GLA, Z_GLA + 2 * GLA_GATE_RANK
    rank = GLA_GATE_RANK
    W["mixin"] = (row(P["mix_norm"]),
                  jnp.concatenate([w_in[:, :g0], _pad_cols(w_in[:, g1:], Z_GATES - Z_GLA),
                                   _pad_cols(w_in[:, g0:g1], LANE)], axis=1).astype(BF16),
                  jnp.pad(P["gla_wa2_fwd"].astype(BF16), ((0, LANE - rank), (0, 0))), row(P["gla_ba_fwd"]),
                  jnp.pad(P["gla_wa2_bwd"].astype(BF16), ((rank, LANE - 2 * rank), (0, 0))), row(P["gla_ba_bwd"]))
    W["gla_norm"] = row(P["gla_out_norm"])
    hp = MLA_HEAD_PAD
    wuq = P["mla_w_uq"].astype(BF16).reshape(MLA_Q_RANK, MLA_HEADS, MLA_QK_HEAD)
    wuq = jnp.pad(wuq, ((0, 0), (0, 0), (0, hp - MLA_QK_HEAD))).reshape(MLA_Q_RANK, MLA_HEADS * hp)
    W["mla"] = (row(P["mla_q_norm"]), wuq, row(P["mla_kv_norm"]), P["mla_w_ukv"].astype(BF16),
                _pad_cols(row(P["mla_qk_q_norm"]), hp), _pad_cols(row(P["mla_qk_k_norm"]), hp))
    W["post1"] = (P["w_out"].astype(BF16), row(P["xattn_norm"]), P["xattn_wq"].astype(BF16),
                  row(P["xattn_q_norm"]))
    W["memkv"] = (row(P["mem_norm"]), P["xattn_wk"].astype(BF16), P["xattn_wv"].astype(BF16),
                  row(P["xattn_k_norm"]))
    W["wo"] = P["xattn_wo"].astype(BF16)
    return W


def _layer(x, mem, W):
    b, s, d = x.shape
    m = mem.shape[1]
    t = b * s

    h = _ffn(x.reshape(t, d), *W["ffn1"])

    z, laf, lab, q, k, v = _mixin(h, s, *W["mixin"], *W["mla"], *_rope_tables(s))
    y_gla = _gla(z.reshape(b, s, Z_GLA), laf.reshape(b, s, GLA_QK), lab.reshape(b, s, GLA_QK), W["gla_norm"])
    y_mla = _flash(q, k, v)

    h, xq = _post1(y_gla.reshape(t, GLA_V), y_mla.reshape(t, MLA_V), h, *W["post1"])
    xk, xv = _memkv(mem.reshape(b * m, d), *W["memkv"])
    h = _post2(xq.reshape(b, s, d), xk.reshape(b, m, d), xv.reshape(b, m, d), h.reshape(b, s, d), W["wo"])

    h = _ffn(h.reshape(t, d), *W["ffn2"])
    return h.reshape(b, s, d)


def kernel(x_prompt, x_sample, mem_prompt, mem_sample, ffn1_norm, ffn1_w_gate, ffn1_w_up, ffn1_w_down, mix_norm, w_in, gla_wa2_fwd, gla_ba_fwd, gla_wa2_bwd, gla_ba_bwd, gla_out_norm, mla_q_norm, mla_w_uq, mla_kv_norm, mla_w_ukv, mla_qk_q_norm, mla_qk_k_norm, w_out, xattn_norm, mem_norm, xattn_wq, xattn_wk, xattn_wv, xattn_q_norm, xattn_k_norm, xattn_wo, ffn2_norm, ffn2_w_gate, ffn2_w_up, ffn2_w_down):
    stacked = dict(
        ffn1_norm=ffn1_norm, ffn1_w_gate=ffn1_w_gate, ffn1_w_up=ffn1_w_up, ffn1_w_down=ffn1_w_down,
        mix_norm=mix_norm, w_in=w_in,
        gla_wa2_fwd=gla_wa2_fwd, gla_ba_fwd=gla_ba_fwd, gla_wa2_bwd=gla_wa2_bwd, gla_ba_bwd=gla_ba_bwd,
        gla_out_norm=gla_out_norm,
        mla_q_norm=mla_q_norm, mla_w_uq=mla_w_uq, mla_kv_norm=mla_kv_norm, mla_w_ukv=mla_w_ukv,
        mla_qk_q_norm=mla_qk_q_norm, mla_qk_k_norm=mla_qk_k_norm,
        w_out=w_out,
        xattn_norm=xattn_norm, mem_norm=mem_norm, xattn_wq=xattn_wq, xattn_wk=xattn_wk, xattn_wv=xattn_wv,
        xattn_q_norm=xattn_q_norm, xattn_k_norm=xattn_k_norm, xattn_wo=xattn_wo,
        ffn2_norm=ffn2_norm, ffn2_w_gate=ffn2_w_gate, ffn2_w_up=ffn2_w_up, ffn2_w_down=ffn2_w_down,
    )
    y_prompt, y_sample = x_prompt, x_sample
    for l in range(ffn1_norm.shape[0]):
        W = _prepare({name: a[l] for name, a in stacked.items()})
        y_prompt = _layer(y_prompt, mem_prompt, W)
        y_sample = _layer(y_sample, mem_sample, W)
    return (y_prompt, y_sample)
```

```python
import functools

import jax
import jax.numpy as jnp
from jax import lax
from jax.experimental import pallas as pl
from jax.experimental.pallas import tpu as pltpu

F32 = jnp.float32
BF16 = jnp.bfloat16

D_MODEL = 2048
D_FF = 5504
MEM_HEADS = 4
MEM_DH = D_MODEL // MEM_HEADS
GLA_HEADS = 4
GLA_DK = 128
GLA_DV = 256
GLA_GATE_RANK = 16
GLA_GATE_TEMP = 16.0
GLA_CHUNK = 64
MLA_HEADS = 8
MLA_Q_RANK = 512
MLA_KV_RANK = 256
MLA_NOPE = 128
MLA_ROPE = 64
MLA_DV = 128
ROPE_THETA = 10000.0
EPS = 1e-6
LOG2_E = 1.4426950408889634
MACARON_STEP = 0.5

GLA_QK = GLA_HEADS * GLA_DK
GLA_V = GLA_HEADS * GLA_DV
MLA_V = MLA_HEADS * MLA_DV
MLA_QK_HEAD = MLA_NOPE + MLA_ROPE

LANE = 128
MXU_DIM = 256
VMEM_LIMIT = 60 * 1024 * 1024

MLA_HEAD_PAD = MXU_DIM
BF16_ROWS = 16
GLA_BLOCK = MXU_DIM
D_FF_TILE = 512
FFN_ROWS = 1024
PROJ_ROWS = 512
MEM_ROWS = 256
PROJ_COLS = 1024
FLASH_Q = 2048
FLASH_K = 1024
Z_GLA = 2 * GLA_QK + 2 * GLA_V
Z_CQ = Z_GLA
Z_CKV = Z_CQ + MLA_Q_RANK
Z_KPE = Z_CKV + MLA_KV_RANK
Z_PAD = -(-(Z_KPE + MLA_ROPE + LANE) // PROJ_COLS) * PROJ_COLS
Z_GATES = Z_PAD - LANE


def _params(*sem):
    return pltpu.CompilerParams(dimension_semantics=sem, vmem_limit_bytes=VMEM_LIMIT)


def _resident(shape):
    return pl.BlockSpec(shape, lambda *_: (0,) * len(shape), pipeline_mode=pl.Buffered(1))


def _rms(x, g):
    return x * lax.rsqrt(jnp.mean(x * x, axis=-1, keepdims=True) + EPS) * g


def _dot(a, b):
    return jnp.dot(a, b, preferred_element_type=F32)


def _dot_nt(a, b):
    return lax.dot_general(a, b, (((1,), (1,)), ((), ())), preferred_element_type=F32)


def _dot_tn(a, b):
    return lax.dot_general(a, b, (((0,), (0,)), ((), ())), preferred_element_type=F32)


def _ffn_kernel(x_ref, g_ref, wg_ref, wu_ref, wd_ref, o_ref, xn_ref, *, overlap):
    j = pl.program_id(1)
    last = pl.num_programs(1) - 1

    @pl.when(j == 0)
    def _():
        x = x_ref[...]
        xn_ref[...] = _rms(x, g_ref[...]).astype(BF16)
        o_ref[...] = x

    xn = xn_ref[...]
    gate = _dot(xn, wg_ref[...])
    up = _dot(xn, wu_ref[...])
    hid = gate * jax.nn.sigmoid(gate) * up
    if overlap:
        col = lax.broadcasted_iota(jnp.int32, (1, hid.shape[1]), 1)
        hid = jnp.where((col >= overlap) | (j != last), hid, 0.0)
    o_ref[...] += _dot(hid.astype(BF16), wd_ref[...])


def _ffn(x, g, wg, wu, wd):
    t, d = x.shape
    f = wg.shape[1]
    tm = min(FFN_ROWS, t)
    tf = D_FF_TILE
    nj = -(-f // tf)
    assert f % LANE == 0 and tf % LANE == 0
    start = lambda j: jnp.minimum(j * (tf // LANE), (f - tf) // LANE) * LANE
    return pl.pallas_call(
        functools.partial(_ffn_kernel, overlap=nj * tf - f),
        grid=(t // tm, nj),
        in_specs=[
            pl.BlockSpec((tm, d), lambda i, j: (i, 0)),
            pl.BlockSpec((1, d), lambda i, j: (0, 0)),
            pl.BlockSpec((pl.Element(d), pl.Element(tf)), lambda i, j: (0, start(j))),
            pl.BlockSpec((pl.Element(d), pl.Element(tf)), lambda i, j: (0, start(j))),
            pl.BlockSpec((pl.Element(tf), pl.Element(d)), lambda i, j: (start(j), 0)),
        ],
        out_specs=pl.BlockSpec((tm, d), lambda i, j: (i, 0)),
        out_shape=jax.ShapeDtypeStruct((t, d), F32),
        scratch_shapes=[pltpu.VMEM((tm, d), BF16)],
        compiler_params=_params("parallel", "arbitrary"),
        name="ffn",
    )(x, g, wg, wu, wd)


def _log_sigmoid(x):
    return jnp.minimum(x, 0.0) - jnp.log(1.0 + jnp.exp(-jnp.abs(x)))


def _mixin_kernel(x_ref, g_ref, w_ref, w2f_ref, bf_ref, w2b_ref, bb_ref,
                  gq_ref, wuq_ref, gkv_ref, wukv_ref, gqq_ref, gqk_ref, cos_ref, sin_ref,
                  z_ref, laf_ref, lab_ref, q_ref, k_ref, v_ref):
    xn = _rms(x_ref[...], g_ref[...]).astype(BF16)
    tail = _dot(xn, w_ref[:, Z_GLA:])
    c = tail[:, Z_GATES - Z_GLA:].astype(BF16)
    laf_ref[...] = _log_sigmoid(_dot(c, w2f_ref[...]) + bf_ref[...]) * (1.0 / GLA_GATE_TEMP)
    lab_ref[...] = _log_sigmoid(_dot(c, w2b_ref[...]) + bb_ref[...]) * (1.0 / GLA_GATE_TEMP)
    _mla_heads(tail[:, Z_CQ - Z_GLA:Z_CKV - Z_GLA], tail[:, Z_CKV - Z_GLA:Z_KPE - Z_GLA],
               tail[:, Z_KPE - Z_GLA:Z_KPE - Z_GLA + LANE],
               gq_ref, wuq_ref, gkv_ref, wukv_ref, gqq_ref, gqk_ref, cos_ref[...], sin_ref[...],
               q_ref, k_ref, v_ref)
    for n in range(Z_GLA // PROJ_COLS):
        sl = slice(n * PROJ_COLS, (n + 1) * PROJ_COLS)
        z_ref[:, sl] = _dot(xn, w_ref[:, sl]).astype(z_ref.dtype)


def _mixin(x, seq, g, w, w2f, bf, w2b, bb, gq, wuq, gkv, wukv, gqq, gqk, cos, sin):
    t, d = x.shape
    tm = min(PROJ_ROWS, seq)
    nb = seq // tm
    hp = MLA_HEAD_PAD
    rows = lambda width: pl.BlockSpec((tm, width), lambda i: (i, 0))
    heads = lambda width: pl.BlockSpec((None, MLA_HEADS, tm, width), lambda i: (i // nb, 0, i % nb, 0))
    table = pl.BlockSpec((tm, LANE), lambda i: (i % nb, 0))
    return pl.pallas_call(
        _mixin_kernel,
        grid=(t // tm,),
        in_specs=[rows(d), _resident((1, d)), _resident(w.shape),
                  _resident((LANE, GLA_QK)), _resident((1, GLA_QK)),
                  _resident((LANE, GLA_QK)), _resident((1, GLA_QK)),
                  _resident((1, MLA_Q_RANK)), _resident(wuq.shape),
                  _resident((1, MLA_KV_RANK)), _resident(wukv.shape),
                  _resident((1, hp)), _resident((1, hp)), table, table],
        out_specs=[rows(Z_GLA), rows(GLA_QK), rows(GLA_QK), heads(hp), heads(hp), heads(MLA_DV)],
        out_shape=[
            jax.ShapeDtypeStruct((t, Z_GLA), BF16),
            jax.ShapeDtypeStruct((t, GLA_QK), F32),
            jax.ShapeDtypeStruct((t, GLA_QK), F32),
            jax.ShapeDtypeStruct((t // seq, MLA_HEADS, seq, hp), BF16),
            jax.ShapeDtypeStruct((t // seq, MLA_HEADS, seq, hp), BF16),
            jax.ShapeDtypeStruct((t // seq, MLA_HEADS, seq, MLA_DV), BF16),
        ],
        compiler_params=_params("parallel"),
        name="mixin",
    )(x, g, w, w2f, bf, w2b, bb, gq, wuq, gkv, wukv, gqq, gqk, cos, sin)


def _gla_kernel(q_ref, k_ref, v_ref, r_ref, laf_ref, lab_ref, gn_ref, o_ref,
                oacc_ref, sf_ref, sb_ref, *, seq):
    c = GLA_CHUNK
    blk = GLA_BLOCK
    cpb = blk // c
    n_blocks = seq // blk
    row = lax.broadcasted_iota(jnp.int32, (blk, blk), 0)
    col = lax.broadcasted_iota(jnp.int32, (blk, blk), 1)
    same_chunk = (row // c) == (col // c)
    lower = same_chunk & (col <= row)
    upper = same_chunk & (col >= row)
    row_in_chunk = lax.broadcasted_iota(jnp.int32, (blk, GLA_DK), 0) % c
    q_scale = GLA_DK ** -0.5

    sf_ref[...] = jnp.zeros_like(sf_ref)
    sb_ref[...] = jnp.zeros_like(sb_ref)

    def chunk_cumsum(x, forward):
        shift = 1
        while shift < c:
            if forward:
                x = x + jnp.where(row_in_chunk >= shift, pltpu.roll(x, shift, 0), 0.0)
            else:
                x = x + jnp.where(row_in_chunk < c - shift, pltpu.roll(x, blk - shift, 0), 0.0)
            shift *= 2
        return x

    def direction(rows, la_ref, mask, forward, st_ref):
        b = chunk_cumsum(la_ref[rows, :], forward)
        last = c - 1 if forward else 0
        b_end = [b[i * c + last:i * c + last + 1, :] for i in range(cpb)]
        bl = jnp.concatenate([jnp.broadcast_to(r, (c, GLA_DK)) for r in b_end], axis=0)
        b_t = [b[i * LANE:(i + 1) * LANE, :].T for i in range(blk // LANE)]
        q = q_ref[rows, :].astype(F32) * q_scale
        k = k_ref[rows, :].astype(F32)
        qt = (q * jnp.exp(b)).astype(BF16)
        kt = (k * jnp.exp(-b)).astype(BF16)
        ks = (k * jnp.exp(bl - b)).astype(BF16)
        v = v_ref[rows, :]
        a = jnp.where(mask, _dot_nt(qt, kt), 0.0).astype(BF16)
        o_intra = _dot(a, v)
        st = st_ref[...]
        outs = [None] * cpb
        for i in (range(cpb) if forward else reversed(range(cpb))):
            sl = slice(i * c, (i + 1) * c)
            end = (i * c + last) % LANE
            outs[i] = o_intra[sl] + _dot(qt[sl], st.astype(BF16))
            st = st * jnp.exp(b_t[i * c // LANE][:, end:end + 1]) + _dot_tn(ks[sl], v[sl])
        st_ref[...] = st
        return jnp.concatenate(outs, axis=0)

    def finish(rows, o):
        o = o * lax.rsqrt(jnp.mean(o * o, axis=-1, keepdims=True) + EPS) * gn_ref[...]
        r = r_ref[rows, :].astype(F32)
        o_ref[rows, :] = (o * (r * jax.nn.sigmoid(r))).astype(o_ref.dtype)

    def rows_of(n):
        return pl.ds(pl.multiple_of(n * blk, blk), blk)

    def first_half(n, carry):
        rf, rb = rows_of(n), rows_of(n_blocks - 1 - n)
        oacc_ref[rf, :] = direction(rf, laf_ref, lower, True, sf_ref)
        oacc_ref[rb, :] = direction(rb, lab_ref, upper, False, sb_ref)
        return carry

    def second_half(n, carry):
        rf, rb = rows_of(n), rows_of(n_blocks - 1 - n)
        finish(rf, oacc_ref[rf, :] + direction(rf, laf_ref, lower, True, sf_ref))
        finish(rb, oacc_ref[rb, :] + direction(rb, lab_ref, upper, False, sb_ref))
        return carry

    lax.fori_loop(0, n_blocks // 2, first_half, 0, unroll=4)
    lax.fori_loop(n_blocks // 2, n_blocks, second_half, 0, unroll=2)


def _gla(z, laf, lab, gn):
    b, s, _ = z.shape
    assert s % (2 * GLA_BLOCK) == 0
    k_blk = GLA_QK // GLA_DK
    v_blk = 2 * GLA_QK // GLA_DV
    r_blk = (2 * GLA_QK + GLA_V) // GLA_DV
    return pl.pallas_call(
        functools.partial(_gla_kernel, seq=s),
        grid=(b, GLA_HEADS),
        in_specs=[
            pl.BlockSpec((None, s, GLA_DK), lambda i, h: (i, 0, h)),
            pl.BlockSpec((None, s, GLA_DK), lambda i, h: (i, 0, k_blk + h)),
            pl.BlockSpec((None, s, GLA_DV), lambda i, h: (i, 0, v_blk + h)),
            pl.BlockSpec((None, s, GLA_DV), lambda i, h: (i, 0, r_blk + h)),
            pl.BlockSpec((None, s, GLA_DK), lambda i, h: (i, 0, h)),
            pl.BlockSpec((None, s, GLA_DK), lambda i, h: (i, 0, h)),
            pl.BlockSpec((1, GLA_DV), lambda i, h: (0, h)),
        ],
        out_specs=pl.BlockSpec((None, s, GLA_DV), lambda i, h: (i, 0, h)),
        out_shape=jax.ShapeDtypeStruct((b, s, GLA_V), BF16),
        scratch_shapes=[
            pltpu.VMEM((s, GLA_DV), F32),
            pltpu.VMEM((GLA_DK, GLA_DV), F32),
            pltpu.VMEM((GLA_DK, GLA_DV), F32),
        ],
        compiler_params=_params("parallel", "parallel"),
        name="gla",
    )(z, z, z, z, laf, lab, gn)


def _rope(x, cos, sin):
    half = MLA_ROPE // 2
    return x * cos - pltpu.roll(x, LANE - half, 1) * sin + pltpu.roll(x, half, 1) * sin


def _mla_heads(cq, ckv, kpe, gq_ref, wuq_ref, gkv_ref, wukv_ref, gqq_ref, gqk_ref, cos, sin,
               q_ref, k_ref, v_ref):
    inv_w = 1.0 / MLA_QK_HEAD
    scale = MLA_QK_HEAD ** -0.5 * LOG2_E
    hp = MLA_HEAD_PAD

    qn = _rms(cq, gq_ref[...]).astype(BF16)
    qf = _dot(qn, wuq_ref[...])
    gqq = gqq_ref[...] * scale
    for h in range(MLA_HEADS):
        qh = qf[:, h * hp:(h + 1) * hp]
        r = lax.rsqrt(jnp.sum(qh * qh, axis=-1, keepdims=True) * inv_w + EPS)
        q_ref[h, :, :MLA_NOPE] = (qh[:, :MLA_NOPE] * gqq[:, :MLA_NOPE] * r).astype(BF16)
        q_ref[h, :, MLA_NOPE:] = (_rope(qh[:, MLA_NOPE:] * gqq[:, MLA_NOPE:], cos, sin) * r).astype(BF16)

    kvn = _rms(ckv, gkv_ref[...]).astype(BF16)
    kvf = _dot(kvn, wukv_ref[...])
    kpe_ss = jnp.sum(kpe * kpe, axis=-1, keepdims=True)
    gqk = gqk_ref[...]
    k_rope = _rope(kpe * gqk[:, MLA_NOPE:], cos, sin)
    hw = MLA_NOPE + MLA_DV
    for h in range(MLA_HEADS):
        kn = kvf[:, h * hw:h * hw + MLA_NOPE]
        r = lax.rsqrt((jnp.sum(kn * kn, axis=-1, keepdims=True) + kpe_ss) * inv_w + EPS)
        k_ref[h, :, :MLA_NOPE] = (kn * gqk[:, :MLA_NOPE] * r).astype(BF16)
        k_ref[h, :, MLA_NOPE:] = (k_rope * r).astype(BF16)
        v_ref[h] = kvf[:, h * hw + MLA_NOPE:(h + 1) * hw].astype(BF16)


def _flash_kernel(q_ref, k_ref, v_ref, *refs, seq, tk, casts):
    o_ref = refs[len(casts)]
    step = (pl.program_id(0) * pl.num_programs(1) + pl.program_id(1)) * pl.num_programs(2) + pl.program_id(2)
    for (scale, n_slabs, every), src, dst in zip(casts, refs[:len(casts)], refs[len(casts) + 1:]):
        def cast(src=src, dst=dst, scale=scale):
            w = src[...]
            dst[...] = (w if scale == 1.0 else w * scale).astype(dst.dtype)
        if every:
            cast()
        else:
            pl.when(step < n_slabs)(cast)
    q = q_ref[...]
    m = l = acc = None
    for n in range(seq // tk):
        s = _dot_nt(q, k_ref[pl.ds(n * tk, tk), :])
        v = v_ref[pl.ds(n * tk, tk), :]
        s_max = jnp.max(s, axis=-1, keepdims=True)
        if n == 0:
            m = s_max
            p = jnp.exp2(s - m)
            l = jnp.sum(p, axis=-1, keepdims=True)
            acc = _dot(p.astype(BF16), v)
        else:
            m_new = jnp.maximum(m, s_max)
            alpha = jnp.exp2(m - m_new)
            p = jnp.exp2(s - m_new)
            l = alpha * l + jnp.sum(p, axis=-1, keepdims=True)
            acc = alpha * acc + _dot(p.astype(BF16), v)
            m = m_new
    o_ref[...] = (acc / l).astype(o_ref.dtype)


def _cast_slabs(shape, n_steps):
    rows, cols = shape
    options = [(n, (rows // n, cols), 0) for n in range(n_steps, 0, -1)
               if n_steps % n == 0 and rows % n == 0 and (rows // n) % BF16_ROWS == 0]
    if cols % LANE == 0 and cols // LANE <= n_steps:
        options.append((cols // LANE, (rows, LANE), 1))
    n_slabs, block, axis = max(options, key=lambda o: o[0])
    return block, axis, n_slabs


def _flash(q, k, v, casts=()):
    b, h, s, hp = q.shape
    tq = min(FLASH_Q, s)
    tk = min(FLASH_K, s)
    nqt = s // tq
    n_steps = b * h * nqt
    step = lambda i, j, n: (i * h + j) * nqt + n
    cast_specs, cast_args = [], []
    for w, scale in casts:
        block, axis, n_slabs = _cast_slabs(w.shape, n_steps)
        slab = lambda i, j, n, n_slabs=n_slabs: jnp.minimum(step(i, j, n), n_slabs - 1)
        index = (lambda *g, slab=slab: (slab(*g), 0)) if axis == 0 else (lambda *g, slab=slab: (0, slab(*g)))
        cast_specs.append(pl.BlockSpec(block, index))
        cast_args.append((scale, n_slabs, n_slabs == n_steps))
    outs = pl.pallas_call(
        functools.partial(_flash_kernel, seq=s, tk=tk, casts=tuple(cast_args)),
        grid=(b, h, nqt),
        in_specs=[
            pl.BlockSpec((None, None, tq, hp), lambda i, j, n: (i, j, n, 0)),
            pl.BlockSpec((None, None, s, hp), lambda i, j, n: (i, j, 0, 0)),
            pl.BlockSpec((None, None, s, MLA_DV), lambda i, j, n: (i, j, 0, 0)),
        ] + cast_specs,
        out_specs=[pl.BlockSpec((None, tq, MLA_DV), lambda i, j, n: (i, n, j))] + cast_specs,
        out_shape=[jax.ShapeDtypeStruct((b, s, h * MLA_DV), BF16)]
                  + [jax.ShapeDtypeStruct(w.shape, BF16) for w, _ in casts],
        compiler_params=_params("arbitrary", "arbitrary", "arbitrary"),
        name="mla_flash",
    )(q, k, v, *[w for w, _ in casts])
    return outs[0], outs[1:]


def _post1_kernel(a_ref, b_ref, res_ref, w_ref, g_ref, wq_ref, hg_ref, h_ref, q_ref):
    ka = a_ref.shape[1]
    h = res_ref[...] + _dot(a_ref[...], w_ref[:ka, :]) + _dot(b_ref[...], w_ref[ka:, :])
    h_ref[...] = h
    xn = _rms(h, g_ref[...]).astype(BF16)
    hg = hg_ref[...] * (MEM_DH ** -0.5 * LOG2_E)
    for n in range(q_ref.shape[1] // PROJ_COLS):
        sl = slice(n * PROJ_COLS, (n + 1) * PROJ_COLS)
        acc = _dot(xn, wq_ref[:, sl])
        for h2 in range(PROJ_COLS // MEM_DH):
            hs = slice(h2 * MEM_DH, (h2 + 1) * MEM_DH)
            q_ref[:, n * PROJ_COLS + h2 * MEM_DH:n * PROJ_COLS + (h2 + 1) * MEM_DH] = (
                _rms(acc[:, hs], hg).astype(q_ref.dtype))


def _post1(a, b, res, w, g, wq, hg):
    t, d = res.shape
    tm = min(PROJ_ROWS, t)
    rows = lambda width: pl.BlockSpec((tm, width), lambda i: (i, 0))
    return pl.pallas_call(
        _post1_kernel,
        grid=(t // tm,),
        in_specs=[rows(a.shape[1]), rows(b.shape[1]), rows(d),
                  _resident(w.shape), _resident((1, d)), _resident(wq.shape), _resident((1, MEM_DH))],
        out_specs=[rows(d), rows(d)],
        out_shape=[jax.ShapeDtypeStruct((t, d), F32), jax.ShapeDtypeStruct((t, d), BF16)],
        compiler_params=_params("parallel"),
        name="post1",
    )(a, b, res, w, g, wq, hg)


def _post2_kernel(q_ref, k_ref, v_ref, res_ref, wo_ref, o_ref, att_ref):
    for h in range(MEM_HEADS):
        sl = slice(h * MEM_DH, (h + 1) * MEM_DH)
        s = _dot_nt(q_ref[:, sl], k_ref[:, sl])
        p = jnp.exp2(s - jnp.max(s, axis=-1, keepdims=True))
        l = jnp.sum(p, axis=-1, keepdims=True)
        att_ref[:, sl] = (_dot(p.astype(BF16), v_ref[:, sl]) / l).astype(BF16)
    o_ref[...] = res_ref[...] + _dot(att_ref[...], wo_ref[...])


def _post2(q, k, v, res, wo):
    b, s, d = q.shape
    m = k.shape[1]
    tm = min(PROJ_ROWS, s)
    tok = pl.BlockSpec((None, tm, d), lambda i, j: (i, j, 0))
    mem = pl.BlockSpec((None, m, d), lambda i, j: (i, 0, 0))
    return pl.pallas_call(
        _post2_kernel,
        grid=(b, s // tm),
        in_specs=[tok, mem, mem, tok, _resident(wo.shape)],
        out_specs=tok,
        out_shape=jax.ShapeDtypeStruct((b, s, d), F32),
        scratch_shapes=[pltpu.VMEM((tm, d), BF16)],
        compiler_params=_params("parallel", "arbitrary"),
        name="post2",
    )(q, k, v, res, wo)


def _memkv_kernel(x_ref, g_ref, wk_ref, wv_ref, hg_ref, k_ref, v_ref):
    xn = _rms(x_ref[...], g_ref[...]).astype(BF16)
    hg = hg_ref[...]
    for n in range(k_ref.shape[1] // PROJ_COLS):
        sl = slice(n * PROJ_COLS, (n + 1) * PROJ_COLS)
        acc = _dot(xn, wk_ref[:, sl])
        for h in range(PROJ_COLS // MEM_DH):
            hs = slice(h * MEM_DH, (h + 1) * MEM_DH)
            k_ref[:, n * PROJ_COLS + h * MEM_DH:n * PROJ_COLS + (h + 1) * MEM_DH] = (
                _rms(acc[:, hs], hg).astype(k_ref.dtype))
        v_ref[:, sl] = _dot(xn, wv_ref[:, sl]).astype(v_ref.dtype)


def _memkv(x, g, wk, wv, hg):
    t, d = x.shape
    tm = min(MEM_ROWS, t)
    rows = pl.BlockSpec((tm, d), lambda i: (i, 0))
    return pl.pallas_call(
        _memkv_kernel,
        grid=(t // tm,),
        in_specs=[rows, _resident((1, d)), _resident(wk.shape), _resident(wv.shape), _resident((1, MEM_DH))],
        out_specs=[rows, rows],
        out_shape=[jax.ShapeDtypeStruct((t, d), BF16), jax.ShapeDtypeStruct((t, d), BF16)],
        compiler_params=_params("parallel"),
        name="mem_kv",
    )(x, g, wk, wv, hg)


def _rope_tables(seq):
    half = MLA_ROPE // 2
    inv = ROPE_THETA ** (-jnp.arange(half, dtype=F32) / half)
    ang = jnp.arange(seq, dtype=jnp.int32).astype(F32)[:, None] * inv[None, :]
    pad = jnp.zeros((seq, LANE - MLA_ROPE), F32)
    cos = jnp.concatenate([jnp.cos(ang), jnp.cos(ang), pad], axis=-1)
    sin = jnp.concatenate([jnp.sin(ang), jnp.sin(ang), pad], axis=-1)
    return cos, sin


def _pad_cols(w, n):
    return jnp.concatenate([w, jnp.zeros((w.shape[0], n - w.shape[1]), w.dtype)], axis=1)


def _prepare(P):
    row = lambda a: a.reshape(1, -1).astype(F32)
    W = {name: row(P[name]) for name in ("ffn2_norm", "xattn_norm", "xattn_q_norm", "mem_norm", "xattn_k_norm")}
    W["ffn1"] = (row(P["ffn1_norm"]), P["ffn1_w_gate"].astype(BF16), P["ffn1_w_up"].astype(BF16),
                 (MACARON_STEP * P["ffn1_w_down"]).astype(BF16))
    w_in = P["w_in"]
    g0, g1 = Z_GLA, Z_GLA + 2 * GLA_GATE_RANK
    rank = GLA_GATE_RANK
    W["mixin"] = (row(P["mix_norm"]),
                  jnp.concatenate([w_in[:, :g0], _pad_cols(w_in[:, g1:], Z_GATES - Z_GLA),
                                   _pad_cols(w_in[:, g0:g1], LANE)], axis=1).astype(BF16),
                  jnp.pad(P["gla_wa2_fwd"].astype(BF16), ((0, LANE - rank), (0, 0))), row(P["gla_ba_fwd"]),
                  jnp.pad(P["gla_wa2_bwd"].astype(BF16), ((rank, LANE - 2 * rank), (0, 0))), row(P["gla_ba_bwd"]))
    W["gla_norm"] = row(P["gla_out_norm"])
    hp = MLA_HEAD_PAD
    wuq = P["mla_w_uq"].astype(BF16).reshape(MLA_Q_RANK, MLA_HEADS, MLA_QK_HEAD)
    wuq = jnp.pad(wuq, ((0, 0), (0, 0), (0, hp - MLA_QK_HEAD))).reshape(MLA_Q_RANK, MLA_HEADS * hp)
    W["mla"] = (row(P["mla_q_norm"]), wuq, row(P["mla_kv_norm"]), P["mla_w_ukv"].astype(BF16),
                _pad_cols(row(P["mla_qk_q_norm"]), hp), _pad_cols(row(P["mla_qk_k_norm"]), hp))
    return W


LATE_WEIGHTS = {"w_out": 1.0, "xattn_wq": 1.0, "xattn_wk": 1.0, "xattn_wv": 1.0, "xattn_wo": 1.0,
                "ffn2_w_gate": 1.0, "ffn2_w_up": 1.0, "ffn2_w_down": MACARON_STEP}


def _layer(x, mem, W, P, late=None):
    b, s, d = x.shape
    m = mem.shape[1]
    t = b * s

    h = _ffn(x.reshape(t, d), *W["ffn1"])

    z, laf, lab, q, k, v = _mixin(h, s, *W["mixin"], *W["mla"], *_rope_tables(s))
    y_gla = _gla(z.reshape(b, s, Z_GLA), laf.reshape(b, s, GLA_QK), lab.reshape(b, s, GLA_QK), W["gla_norm"])
    if late is None:
        y_mla, copies = _flash(q, k, v, [(P[name], scale) for name, scale in LATE_WEIGHTS.items()])
        late = dict(zip(LATE_WEIGHTS, copies))
    else:
        y_mla, _ = _flash(q, k, v)

    h, xq = _post1(y_gla.reshape(t, GLA_V), y_mla.reshape(t, MLA_V), h, late["w_out"], W["xattn_norm"],
                   late["xattn_wq"], W["xattn_q_norm"])
    xk, xv = _memkv(mem.reshape(b * m, d), W["mem_norm"], late["xattn_wk"], late["xattn_wv"], W["xattn_k_norm"])
    h = _post2(xq.reshape(b, s, d), xk.reshape(b, m, d), xv.reshape(b, m, d), h.reshape(b, s, d),
               late["xattn_wo"])

    h = _ffn(h.reshape(t, d), W["ffn2_norm"], late["ffn2_w_gate"], late["ffn2_w_up"], late["ffn2_w_down"])
    return h.reshape(b, s, d), late


def kernel(x_prompt, x_sample, mem_prompt, mem_sample, ffn1_norm, ffn1_w_gate, ffn1_w_up, ffn1_w_down, mix_norm, w_in, gla_wa2_fwd, gla_ba_fwd, gla_wa2_bwd, gla_ba_bwd, gla_out_norm, mla_q_norm, mla_w_uq, mla_kv_norm, mla_w_ukv, mla_qk_q_norm, mla_qk_k_norm, w_out, xattn_norm, mem_norm, xattn_wq, xattn_wk, xattn_wv, xattn_q_norm, xattn_k_norm, xattn_wo, ffn2_norm, ffn2_w_gate, ffn2_w_up, ffn2_w_down):
    stacked = dict(
        ffn1_norm=ffn1_norm, ffn1_w_gate=ffn1_w_gate, ffn1_w_up=ffn1_w_up, ffn1_w_down=ffn1_w_down,
        mix_norm=mix_norm, w_in=w_in,
        gla_wa2_fwd=gla_wa2_fwd, gla_ba_fwd=gla_ba_fwd, gla_wa2_bwd=gla_wa2_bwd, gla_ba_bwd=gla_ba_bwd,
        gla_out_norm=gla_out_norm,
        mla_q_norm=mla_q_norm, mla_w_uq=mla_w_uq, mla_kv_norm=mla_kv_norm, mla_w_ukv=mla_w_ukv,
        mla_qk_q_norm=mla_qk_q_norm, mla_qk_k_norm=mla_qk_k_norm,
        w_out=w_out,
        xattn_norm=xattn_norm, mem_norm=mem_norm, xattn_wq=xattn_wq, xattn_wk=xattn_wk, xattn_wv=xattn_wv,
        xattn_q_norm=xattn_q_norm, xattn_k_norm=xattn_k_norm, xattn_wo=xattn_wo,
        ffn2_norm=ffn2_norm, ffn2_w_gate=ffn2_w_gate, ffn2_w_up=ffn2_w_up, ffn2_w_down=ffn2_w_down,
    )
    y_prompt, y_sample = x_prompt, x_sample
    for l in range(ffn1_norm.shape[0]):
        P = {name: a[l] for name, a in stacked.items()}
        W = _prepare(P)
        y_prompt, late = _layer(y_prompt, mem_prompt, W, P)
        y_sample, _ = _layer(y_sample, mem_sample, W, P, late)
    return (y_prompt, y_sample)
```

```python
import functools

import jax
import jax.numpy as jnp
from jax import lax
from jax.experimental import pallas as pl
from jax.experimental.pallas import tpu as pltpu

F32 = jnp.float32
BF16 = jnp.bfloat16

D_MODEL = 2048
D_FF = 5504
MEM_HEADS = 4
MEM_DH = D_MODEL // MEM_HEADS
GLA_HEADS = 4
GLA_DK = 128
GLA_DV = 256
GLA_GATE_RANK = 16
GLA_GATE_TEMP = 16.0
GLA_CHUNK = 64
MLA_HEADS = 8
MLA_Q_RANK = 512
MLA_KV_RANK = 256
MLA_NOPE = 128
MLA_ROPE = 64
MLA_DV = 128
ROPE_THETA = 10000.0
EPS = 1e-6
LOG2_E = 1.4426950408889634
MACARON_STEP = 0.5

GLA_QK = GLA_HEADS * GLA_DK
GLA_V = GLA_HEADS * GLA_DV
MLA_V = MLA_HEADS * MLA_DV
MLA_QK_HEAD = MLA_NOPE + MLA_ROPE

LANE = 128
MXU_DIM = 256
VMEM_LIMIT = 60 * 1024 * 1024

MLA_HEAD_PAD = MXU_DIM
BF16_ROWS = 16
GLA_BLOCK = MXU_DIM
D_FF_TILE = 512
FFN_ROWS = 1024
PROJ_ROWS = 512
MEM_ROWS = 256
PROJ_COLS = 1024
FLASH_Q = 2048
FLASH_K = 1024
Z_GLA = 2 * GLA_QK + 2 * GLA_V
Z_CQ = Z_GLA
Z_CKV = Z_CQ + MLA_Q_RANK
Z_KPE = Z_CKV + MLA_KV_RANK
Z_PAD = -(-(Z_KPE + MLA_ROPE + LANE) // PROJ_COLS) * PROJ_COLS
Z_GATES = Z_PAD - LANE


def _params(*sem):
    return pltpu.CompilerParams(dimension_semantics=sem, vmem_limit_bytes=VMEM_LIMIT)


def _resident(shape):
    return pl.BlockSpec(shape, lambda *_: (0,) * len(shape), pipeline_mode=pl.Buffered(1))


def _rms(x, g):
    return x * lax.rsqrt(jnp.mean(x * x, axis=-1, keepdims=True) + EPS) * g


def _dot(a, b):
    return jnp.dot(a, b, preferred_element_type=F32)


def _dot_nt(a, b):
    return lax.dot_general(a, b, (((1,), (1,)), ((), ())), preferred_element_type=F32)


def _dot_tn(a, b):
    return lax.dot_general(a, b, (((0,), (0,)), ((), ())), preferred_element_type=F32)


def _ffn_kernel(x_ref, g_ref, wg_ref, wu_ref, wd_ref, o_ref, xn_ref, *, overlap):
    j = pl.program_id(1)
    last = pl.num_programs(1) - 1

    @pl.when(j == 0)
    def _():
        x = x_ref[...]
        xn_ref[...] = _rms(x, g_ref[...]).astype(BF16)
        o_ref[...] = x

    xn = xn_ref[...]
    gate = _dot(xn, wg_ref[...])
    up = _dot(xn, wu_ref[...])
    hid = gate * jax.nn.sigmoid(gate) * up
    if overlap:
        col = lax.broadcasted_iota(jnp.int32, (1, hid.shape[1]), 1)
        hid = jnp.where((col >= overlap) | (j != last), hid, 0.0)
    o_ref[...] += _dot(hid.astype(BF16), wd_ref[...])


def _ffn(x, g, wg, wu, wd):
    t, d = x.shape
    f = wg.shape[1]
    tm = min(FFN_ROWS, t)
    tf = D_FF_TILE
    nj = -(-f // tf)
    assert f % LANE == 0 and tf % LANE == 0
    start = lambda j: jnp.minimum(j * (tf // LANE), (f - tf) // LANE) * LANE
    return pl.pallas_call(
        functools.partial(_ffn_kernel, overlap=nj * tf - f),
        grid=(t // tm, nj),
        in_specs=[
            pl.BlockSpec((tm, d), lambda i, j: (i, 0)),
            pl.BlockSpec((1, d), lambda i, j: (0, 0)),
            pl.BlockSpec((pl.Element(d), pl.Element(tf)), lambda i, j: (0, start(j))),
            pl.BlockSpec((pl.Element(d), pl.Element(tf)), lambda i, j: (0, start(j))),
            pl.BlockSpec((pl.Element(tf), pl.Element(d)), lambda i, j: (start(j), 0)),
        ],
        out_specs=pl.BlockSpec((tm, d), lambda i, j: (i, 0)),
        out_shape=jax.ShapeDtypeStruct((t, d), F32),
        scratch_shapes=[pltpu.VMEM((tm, d), BF16)],
        compiler_params=_params("parallel", "arbitrary"),
        name="ffn",
    )(x, g, wg, wu, wd)


def _log_sigmoid(x):
    return jnp.minimum(x, 0.0) - jnp.log(1.0 + jnp.exp(-jnp.abs(x)))


def _mixin_kernel(x_ref, g_ref, w_ref, wt_ref, w2f_ref, bf_ref, w2b_ref, bb_ref,
                  gq_ref, wuq_ref, gkv_ref, wukv_ref, gqq_ref, gqk_ref, cos_ref, sin_ref,
                  z_ref, laf_ref, lab_ref, q_ref, k_ref, v_ref):
    xn = _rms(x_ref[...], g_ref[...]).astype(BF16)
    tail = _dot(xn, wt_ref[...])
    c = tail[:, Z_GATES - Z_GLA:].astype(BF16)
    laf_ref[...] = _log_sigmoid(_dot(c, w2f_ref[...]) + bf_ref[...]) * (1.0 / GLA_GATE_TEMP)
    lab_ref[...] = _log_sigmoid(_dot(c, w2b_ref[...]) + bb_ref[...]) * (1.0 / GLA_GATE_TEMP)
    _mla_heads(tail[:, Z_CQ - Z_GLA:Z_CKV - Z_GLA], tail[:, Z_CKV - Z_GLA:Z_KPE - Z_GLA],
               tail[:, Z_KPE - Z_GLA:Z_KPE - Z_GLA + LANE],
               gq_ref, wuq_ref, gkv_ref, wukv_ref, gqq_ref, gqk_ref, cos_ref[...], sin_ref[...],
               q_ref, k_ref, v_ref)
    for n in range(Z_GLA // PROJ_COLS):
        sl = slice(n * PROJ_COLS, (n + 1) * PROJ_COLS)
        z_ref[:, sl] = _dot(xn, w_ref[:, sl]).astype(z_ref.dtype)


def _mixin(x, seq, g, w, wt, w2f, bf, w2b, bb, gq, wuq, gkv, wukv, gqq, gqk, cos, sin):
    t, d = x.shape
    tm = min(PROJ_ROWS, seq)
    nb = seq // tm
    hp = MLA_HEAD_PAD
    rows = lambda width: pl.BlockSpec((tm, width), lambda i: (i, 0))
    heads = lambda width: pl.BlockSpec((None, MLA_HEADS, tm, width), lambda i: (i // nb, 0, i % nb, 0))
    table = pl.BlockSpec((tm, LANE), lambda i: (i % nb, 0))
    return pl.pallas_call(
        _mixin_kernel,
        grid=(t // tm,),
        in_specs=[rows(d), _resident((1, d)), _resident((d, Z_GLA)), _resident(wt.shape),
                  _resident((LANE, GLA_QK)), _resident((1, GLA_QK)),
                  _resident((LANE, GLA_QK)), _resident((1, GLA_QK)),
                  _resident((1, MLA_Q_RANK)), _resident(wuq.shape),
                  _resident((1, MLA_KV_RANK)), _resident(wukv.shape),
                  _resident((1, hp)), _resident((1, hp)), table, table],
        out_specs=[rows(Z_GLA), rows(GLA_QK), rows(GLA_QK), heads(hp), heads(hp), heads(MLA_DV)],
        out_shape=[
            jax.ShapeDtypeStruct((t, Z_GLA), BF16),
            jax.ShapeDtypeStruct((t, GLA_QK), F32),
            jax.ShapeDtypeStruct((t, GLA_QK), F32),
            jax.ShapeDtypeStruct((t // seq, MLA_HEADS, seq, hp), BF16),
            jax.ShapeDtypeStruct((t // seq, MLA_HEADS, seq, hp), BF16),
            jax.ShapeDtypeStruct((t // seq, MLA_HEADS, seq, MLA_DV), BF16),
        ],
        compiler_params=_params("parallel"),
        name="mixin",
    )(x, g, w, wt, w2f, bf, w2b, bb, gq, wuq, gkv, wukv, gqq, gqk, cos, sin)


def _gla_kernel(q_ref, k_ref, v_ref, r_ref, laf_ref, lab_ref, gn_ref, o_ref,
                oacc_ref, sf_ref, sb_ref, *, seq):
    c = GLA_CHUNK
    blk = GLA_BLOCK
    cpb = blk // c
    n_blocks = seq // blk
    row = lax.broadcasted_iota(jnp.int32, (blk, blk), 0)
    col = lax.broadcasted_iota(jnp.int32, (blk, blk), 1)
    same_chunk = (row // c) == (col // c)
    lower = same_chunk & (col <= row)
    upper = same_chunk & (col >= row)
    row_in_chunk = lax.broadcasted_iota(jnp.int32, (blk, GLA_DK), 0) % c
    q_scale = GLA_DK ** -0.5

    sf_ref[...] = jnp.zeros_like(sf_ref)
    sb_ref[...] = jnp.zeros_like(sb_ref)

    def chunk_cumsum(x, forward):
        shift = 1
        while shift < c:
            if forward:
                x = x + jnp.where(row_in_chunk >= shift, pltpu.roll(x, shift, 0), 0.0)
            else:
                x = x + jnp.where(row_in_chunk < c - shift, pltpu.roll(x, blk - shift, 0), 0.0)
            shift *= 2
        return x

    def direction(rows, la_ref, mask, forward, st_ref):
        b = chunk_cumsum(la_ref[rows, :], forward)
        last = c - 1 if forward else 0
        b_end = [b[i * c + last:i * c + last + 1, :] for i in range(cpb)]
        bl = jnp.concatenate([jnp.broadcast_to(r, (c, GLA_DK)) for r in b_end], axis=0)
        b_t = [b[i * LANE:(i + 1) * LANE, :].T for i in range(blk // LANE)]
        q = q_ref[rows, :].astype(F32) * q_scale
        k = k_ref[rows, :].astype(F32)
        qt = (q * jnp.exp(b)).astype(BF16)
        kt = (k * jnp.exp(-b)).astype(BF16)
        ks = (k * jnp.exp(bl - b)).astype(BF16)
        v = v_ref[rows, :]
        a = jnp.where(mask, _dot_nt(qt, kt), 0.0).astype(BF16)
        o_intra = _dot(a, v)
        st = st_ref[...]
        outs = [None] * cpb
        for i in (range(cpb) if forward else reversed(range(cpb))):
            sl = slice(i * c, (i + 1) * c)
            end = (i * c + last) % LANE
            outs[i] = o_intra[sl] + _dot(qt[sl], st.astype(BF16))
            st = st * jnp.exp(b_t[i * c // LANE][:, end:end + 1]) + _dot_tn(ks[sl], v[sl])
        st_ref[...] = st
        return jnp.concatenate(outs, axis=0)

    def finish(rows, o):
        o = o * lax.rsqrt(jnp.mean(o * o, axis=-1, keepdims=True) + EPS) * gn_ref[...]
        r = r_ref[rows, :].astype(F32)
        o_ref[rows, :] = (o * (r * jax.nn.sigmoid(r))).astype(o_ref.dtype)

    def rows_of(n):
        return pl.ds(pl.multiple_of(n * blk, blk), blk)

    def first_half(n, carry):
        rf, rb = rows_of(n), rows_of(n_blocks - 1 - n)
        oacc_ref[rf, :] = direction(rf, laf_ref, lower, True, sf_ref)
        oacc_ref[rb, :] = direction(rb, lab_ref, upper, False, sb_ref)
        return carry

    def second_half(n, carry):
        rf, rb = rows_of(n), rows_of(n_blocks - 1 - n)
        finish(rf, oacc_ref[rf, :] + direction(rf, laf_ref, lower, True, sf_ref))
        finish(rb, oacc_ref[rb, :] + direction(rb, lab_ref, upper, False, sb_ref))
        return carry

    lax.fori_loop(0, n_blocks // 2, first_half, 0, unroll=4)
    lax.fori_loop(n_blocks // 2, n_blocks, second_half, 0, unroll=2)


def _gla(z, laf, lab, gn):
    b, s, _ = z.shape
    assert s % (2 * GLA_BLOCK) == 0
    k_blk = GLA_QK // GLA_DK
    v_blk = 2 * GLA_QK // GLA_DV
    r_blk = (2 * GLA_QK + GLA_V) // GLA_DV
    return pl.pallas_call(
        functools.partial(_gla_kernel, seq=s),
        grid=(b, GLA_HEADS),
        in_specs=[
            pl.BlockSpec((None, s, GLA_DK), lambda i, h: (i, 0, h)),
            pl.BlockSpec((None, s, GLA_DK), lambda i, h: (i, 0, k_blk + h)),
            pl.BlockSpec((None, s, GLA_DV), lambda i, h: (i, 0, v_blk + h)),
            pl.BlockSpec((None, s, GLA_DV), lambda i, h: (i, 0, r_blk + h)),
            pl.BlockSpec((None, s, GLA_DK), lambda i, h: (i, 0, h)),
            pl.BlockSpec((None, s, GLA_DK), lambda i, h: (i, 0, h)),
            pl.BlockSpec((1, GLA_DV), lambda i, h: (0, h)),
        ],
        out_specs=pl.BlockSpec((None, s, GLA_DV), lambda i, h: (i, 0, h)),
        out_shape=jax.ShapeDtypeStruct((b, s, GLA_V), BF16),
        scratch_shapes=[
            pltpu.VMEM((s, GLA_DV), F32),
            pltpu.VMEM((GLA_DK, GLA_DV), F32),
            pltpu.VMEM((GLA_DK, GLA_DV), F32),
        ],
        compiler_params=_params("parallel", "parallel"),
        name="gla",
    )(z, z, z, z, laf, lab, gn)


def _rope(x, cos, sin):
    half = MLA_ROPE // 2
    return x * cos - pltpu.roll(x, LANE - half, 1) * sin + pltpu.roll(x, half, 1) * sin


def _mla_heads(cq, ckv, kpe, gq_ref, wuq_ref, gkv_ref, wukv_ref, gqq_ref, gqk_ref, cos, sin,
               q_ref, k_ref, v_ref):
    inv_w = 1.0 / MLA_QK_HEAD
    scale = MLA_QK_HEAD ** -0.5 * LOG2_E
    hp = MLA_HEAD_PAD

    qn = _rms(cq, gq_ref[...]).astype(BF16)
    qf = _dot(qn, wuq_ref[...])
    gqq = gqq_ref[...] * scale
    for h in range(MLA_HEADS):
        qh = qf[:, h * hp:(h + 1) * hp]
        r = lax.rsqrt(jnp.sum(qh * qh, axis=-1, keepdims=True) * inv_w + EPS)
        q_ref[h, :, :MLA_NOPE] = (qh[:, :MLA_NOPE] * gqq[:, :MLA_NOPE] * r).astype(BF16)
        q_ref[h, :, MLA_NOPE:] = (_rope(qh[:, MLA_NOPE:] * gqq[:, MLA_NOPE:], cos, sin) * r).astype(BF16)

    kvn = _rms(ckv, gkv_ref[...]).astype(BF16)
    kvf = _dot(kvn, wukv_ref[...])
    kpe_ss = jnp.sum(kpe * kpe, axis=-1, keepdims=True)
    gqk = gqk_ref[...]
    k_rope = _rope(kpe * gqk[:, MLA_NOPE:], cos, sin)
    hw = MLA_NOPE + MLA_DV
    for h in range(MLA_HEADS):
        kn = kvf[:, h * hw:h * hw + MLA_NOPE]
        r = lax.rsqrt((jnp.sum(kn * kn, axis=-1, keepdims=True) + kpe_ss) * inv_w + EPS)
        k_ref[h, :, :MLA_NOPE] = (kn * gqk[:, :MLA_NOPE] * r).astype(BF16)
        k_ref[h, :, MLA_NOPE:] = (k_rope * r).astype(BF16)
        v_ref[h] = kvf[:, h * hw + MLA_NOPE:(h + 1) * hw].astype(BF16)


def _flash_kernel(q_ref, k_ref, v_ref, *refs, seq, tk, casts):
    o_ref = refs[len(casts)]
    step = (pl.program_id(0) * pl.num_programs(1) + pl.program_id(1)) * pl.num_programs(2) + pl.program_id(2)
    for (scale, n_slabs, every), src, dst in zip(casts, refs[:len(casts)], refs[len(casts) + 1:]):
        def cast(src=src, dst=dst, scale=scale):
            w = src[...]
            dst[...] = (w if scale == 1.0 else w * scale).astype(dst.dtype)
        if every:
            cast()
        else:
            pl.when(step < n_slabs)(cast)
    q = q_ref[...]
    m = l = acc = None
    for n in range(seq // tk):
        s = _dot_nt(q, k_ref[pl.ds(n * tk, tk), :])
        v = v_ref[pl.ds(n * tk, tk), :]
        s_max = jnp.max(s, axis=-1, keepdims=True)
        if n == 0:
            m = s_max
            p = jnp.exp2(s - m)
            l = jnp.sum(p, axis=-1, keepdims=True)
            acc = _dot(p.astype(BF16), v)
        else:
            m_new = jnp.maximum(m, s_max)
            alpha = jnp.exp2(m - m_new)
            p = jnp.exp2(s - m_new)
            l = alpha * l + jnp.sum(p, axis=-1, keepdims=True)
            acc = alpha * acc + _dot(p.astype(BF16), v)
            m = m_new
    o_ref[...] = (acc / l).astype(o_ref.dtype)


def _cast_slabs(shape, n_steps):
    rows, cols = shape
    options = [(n, (rows // n, cols), 0) for n in range(n_steps, 0, -1)
               if n_steps % n == 0 and rows % n == 0 and (rows // n) % BF16_ROWS == 0]
    if cols % LANE == 0 and cols // LANE <= n_steps:
        options.append((cols // LANE, (rows, LANE), 1))
    n_slabs, block, axis = max(options, key=lambda o: o[0])
    return block, axis, n_slabs


def _flash(q, k, v, casts=()):
    b, h, s, hp = q.shape
    tq = min(FLASH_Q, s)
    tk = min(FLASH_K, s)
    nqt = s // tq
    n_steps = b * h * nqt
    step = lambda i, j, n: (i * h + j) * nqt + n
    cast_specs, cast_args = [], []
    for w, scale in casts:
        block, axis, n_slabs = _cast_slabs(w.shape, n_steps)
        slab = lambda i, j, n, n_slabs=n_slabs: jnp.minimum(step(i, j, n), n_slabs - 1)
        index = (lambda *g, slab=slab: (slab(*g), 0)) if axis == 0 else (lambda *g, slab=slab: (0, slab(*g)))
        cast_specs.append(pl.BlockSpec(block, index))
        cast_args.append((scale, n_slabs, n_slabs == n_steps))
    outs = pl.pallas_call(
        functools.partial(_flash_kernel, seq=s, tk=tk, casts=tuple(cast_args)),
        grid=(b, h, nqt),
        in_specs=[
            pl.BlockSpec((None, None, tq, hp), lambda i, j, n: (i, j, n, 0)),
            pl.BlockSpec((None, None, s, hp), lambda i, j, n: (i, j, 0, 0)),
            pl.BlockSpec((None, None, s, MLA_DV), lambda i, j, n: (i, j, 0, 0)),
        ] + cast_specs,
        out_specs=[pl.BlockSpec((None, tq, MLA_DV), lambda i, j, n: (i, n, j))] + cast_specs,
        out_shape=[jax.ShapeDtypeStruct((b, s, h * MLA_DV), BF16)]
                  + [jax.ShapeDtypeStruct(w.shape, BF16) for w, _ in casts],
        compiler_params=_params("arbitrary", "arbitrary", "arbitrary"),
        name="mla_flash",
    )(q, k, v, *[w for w, _ in casts])
    return outs[0], outs[1:]


def _post1_kernel(a_ref, b_ref, res_ref, w_ref, g_ref, wq_ref, hg_ref, h_ref, q_ref):
    ka = a_ref.shape[1]
    h = res_ref[...] + _dot(a_ref[...], w_ref[:ka, :]) + _dot(b_ref[...], w_ref[ka:, :])
    h_ref[...] = h
    xn = _rms(h, g_ref[...]).astype(BF16)
    hg = hg_ref[...] * (MEM_DH ** -0.5 * LOG2_E)
    for n in range(q_ref.shape[1] // PROJ_COLS):
        sl = slice(n * PROJ_COLS, (n + 1) * PROJ_COLS)
        acc = _dot(xn, wq_ref[:, sl])
        for h2 in range(PROJ_COLS // MEM_DH):
            hs = slice(h2 * MEM_DH, (h2 + 1) * MEM_DH)
            q_ref[:, n * PROJ_COLS + h2 * MEM_DH:n * PROJ_COLS + (h2 + 1) * MEM_DH] = (
                _rms(acc[:, hs], hg).astype(q_ref.dtype))


def _post1(a, b, res, w, g, wq, hg):
    t, d = res.shape
    tm = min(PROJ_ROWS, t)
    rows = lambda width: pl.BlockSpec((tm, width), lambda i: (i, 0))
    return pl.pallas_call(
        _post1_kernel,
        grid=(t // tm,),
        in_specs=[rows(a.shape[1]), rows(b.shape[1]), rows(d),
                  _resident(w.shape), _resident((1, d)), _resident(wq.shape), _resident((1, MEM_DH))],
        out_specs=[rows(d), rows(d)],
        out_shape=[jax.ShapeDtypeStruct((t, d), F32), jax.ShapeDtypeStruct((t, d), BF16)],
        compiler_params=_params("parallel"),
        name="post1",
    )(a, b, res, w, g, wq, hg)


def _post2_kernel(q_ref, k_ref, v_ref, res_ref, wo_ref, o_ref, att_ref):
    for h in range(MEM_HEADS):
        sl = slice(h * MEM_DH, (h + 1) * MEM_DH)
        s = _dot_nt(q_ref[:, sl], k_ref[:, sl])
        p = jnp.exp2(s - jnp.max(s, axis=-1, keepdims=True))
        l = jnp.sum(p, axis=-1, keepdims=True)
        att_ref[:, sl] = (_dot(p.astype(BF16), v_ref[:, sl]) / l).astype(BF16)
    o_ref[...] = res_ref[...] + _dot(att_ref[...], wo_ref[...])


def _post2(q, k, v, res, wo):
    b, s, d = q.shape
    m = k.shape[1]
    tm = min(PROJ_ROWS, s)
    tok = pl.BlockSpec((None, tm, d), lambda i, j: (i, j, 0))
    mem = pl.BlockSpec((None, m, d), lambda i, j: (i, 0, 0))
    return pl.pallas_call(
        _post2_kernel,
        grid=(b, s // tm),
        in_specs=[tok, mem, mem, tok, _resident(wo.shape)],
        out_specs=tok,
        out_shape=jax.ShapeDtypeStruct((b, s, d), F32),
        scratch_shapes=[pltpu.VMEM((tm, d), BF16)],
        compiler_params=_params("parallel", "arbitrary"),
        name="post2",
    )(q, k, v, res, wo)


def _memkv_kernel(x_ref, g_ref, wk_ref, wv_ref, hg_ref, k_ref, v_ref):
    xn = _rms(x_ref[...], g_ref[...]).astype(BF16)
    hg = hg_ref[...]
    for n in range(k_ref.shape[1] // PROJ_COLS):
        sl = slice(n * PROJ_COLS, (n + 1) * PROJ_COLS)
        acc = _dot(xn, wk_ref[:, sl])
        for h in range(PROJ_COLS // MEM_DH):
            hs = slice(h * MEM_DH, (h + 1) * MEM_DH)
            k_ref[:, n * PROJ_COLS + h * MEM_DH:n * PROJ_COLS + (h + 1) * MEM_DH] = (
                _rms(acc[:, hs], hg).astype(k_ref.dtype))
        v_ref[:, sl] = _dot(xn, wv_ref[:, sl]).astype(v_ref.dtype)


def _memkv(x, g, wk, wv, hg):
    t, d = x.shape
    tm = min(MEM_ROWS, t)
    rows = pl.BlockSpec((tm, d), lambda i: (i, 0))
    return pl.pallas_call(
        _memkv_kernel,
        grid=(t // tm,),
        in_specs=[rows, _resident((1, d)), _resident(wk.shape), _resident(wv.shape), _resident((1, MEM_DH))],
        out_specs=[rows, rows],
        out_shape=[jax.ShapeDtypeStruct((t, d), BF16), jax.ShapeDtypeStruct((t, d), BF16)],
        compiler_params=_params("parallel"),
        name="mem_kv",
    )(x, g, wk, wv, hg)


def _rope_tables(seq):
    half = MLA_ROPE // 2
    inv = ROPE_THETA ** (-jnp.arange(half, dtype=F32) / half)
    ang = jnp.arange(seq, dtype=jnp.int32).astype(F32)[:, None] * inv[None, :]
    pad = jnp.zeros((seq, LANE - MLA_ROPE), F32)
    cos = jnp.concatenate([jnp.cos(ang), jnp.cos(ang), pad], axis=-1)
    sin = jnp.concatenate([jnp.sin(ang), jnp.sin(ang), pad], axis=-1)
    return cos, sin


def _pad_cols(w, n):
    return jnp.concatenate([w, jnp.zeros((w.shape[0], n - w.shape[1]), w.dtype)], axis=1)


def _prepare(P):
    row = lambda a: a.reshape(1, -1).astype(F32)
    W = {name: row(P[name]) for name in ("ffn2_norm", "xattn_norm", "xattn_q_norm", "mem_norm", "xattn_k_norm")}
    W["ffn1"] = (row(P["ffn1_norm"]), P["ffn1_w_gate"].astype(BF16), P["ffn1_w_up"].astype(BF16),
                 (MACARON_STEP * P["ffn1_w_down"]).astype(BF16))
    w_in = P["w_in"].astype(BF16)
    g0, g1 = Z_GLA, Z_GLA + 2 * GLA_GATE_RANK
    rank = GLA_GATE_RANK
    W["mixin"] = (row(P["mix_norm"]), w_in,
                  jnp.concatenate([_pad_cols(w_in[:, g1:], Z_GATES - Z_GLA), _pad_cols(w_in[:, g0:g1], LANE)], axis=1),
                  jnp.pad(P["gla_wa2_fwd"].astype(BF16), ((0, LANE - rank), (0, 0))), row(P["gla_ba_fwd"]),
                  jnp.pad(P["gla_wa2_bwd"].astype(BF16), ((rank, LANE - 2 * rank), (0, 0))), row(P["gla_ba_bwd"]))
    W["gla_norm"] = row(P["gla_out_norm"])
    hp = MLA_HEAD_PAD
    wuq = P["mla_w_uq"].astype(BF16).reshape(MLA_Q_RANK, MLA_HEADS, MLA_QK_HEAD)
    wuq = jnp.pad(wuq, ((0, 0), (0, 0), (0, hp - MLA_QK_HEAD))).reshape(MLA_Q_RANK, MLA_HEADS * hp)
    W["mla"] = (row(P["mla_q_norm"]), wuq, row(P["mla_kv_norm"]), P["mla_w_ukv"].astype(BF16),
                _pad_cols(row(P["mla_qk_q_norm"]), hp), _pad_cols(row(P["mla_qk_k_norm"]), hp))
    return W


LATE_WEIGHTS = {"w_out": 1.0, "xattn_wq": 1.0, "xattn_wk": 1.0, "xattn_wv": 1.0, "xattn_wo": 1.0,
                "ffn2_w_gate": 1.0, "ffn2_w_up": 1.0, "ffn2_w_down": MACARON_STEP}


def _layer(x, mem, W, P, late=None):
    b, s, d = x.shape
    m = mem.shape[1]
    t = b * s

    h = _ffn(x.reshape(t, d), *W["ffn1"])

    z, laf, lab, q, k, v = _mixin(h, s, *W["mixin"], *W["mla"], *_rope_tables(s))
    y_gla = _gla(z.reshape(b, s, Z_GLA), laf.reshape(b, s, GLA_QK), lab.reshape(b, s, GLA_QK), W["gla_norm"])
    if late is None:
        y_mla, copies = _flash(q, k, v, [(P[name], scale) for name, scale in LATE_WEIGHTS.items()])
        late = dict(zip(LATE_WEIGHTS, copies))
    else:
        y_mla, _ = _flash(q, k, v)

    h, xq = _post1(y_gla.reshape(t, GLA_V), y_mla.reshape(t, MLA_V), h, late["w_out"], W["xattn_norm"],
                   late["xattn_wq"], W["xattn_q_norm"])
    xk, xv = _memkv(mem.reshape(b * m, d), W["mem_norm"], late["xattn_wk"], late["xattn_wv"], W["xattn_k_norm"])
    h = _post2(xq.reshape(b, s, d), xk.reshape(b, m, d), xv.reshape(b, m, d), h.reshape(b, s, d),
               late["xattn_wo"])

    h = _ffn(h.reshape(t, d), W["ffn2_norm"], late["ffn2_w_gate"], late["ffn2_w_up"], late["ffn2_w_down"])
    return h.reshape(b, s, d), late


def kernel(x_prompt, x_sample, mem_prompt, mem_sample, ffn1_norm, ffn1_w_gate, ffn1_w_up, ffn1_w_down, mix_norm, w_in, gla_wa2_fwd, gla_ba_fwd, gla_wa2_bwd, gla_ba_bwd, gla_out_norm, mla_q_norm, mla_w_uq, mla_kv_norm, mla_w_ukv, mla_qk_q_norm, mla_qk_k_norm, w_out, xattn_norm, mem_norm, xattn_wq, xattn_wk, xattn_wv, xattn_q_norm, xattn_k_norm, xattn_wo, ffn2_norm, ffn2_w_gate, ffn2_w_up, ffn2_w_down):
    stacked = dict(
        ffn1_norm=ffn1_norm, ffn1_w_gate=ffn1_w_gate, ffn1_w_up=ffn1_w_up, ffn1_w_down=ffn1_w_down,
        mix_norm=mix_norm, w_in=w_in,
        gla_wa2_fwd=gla_wa2_fwd, gla_ba_fwd=gla_ba_fwd, gla_wa2_bwd=gla_wa2_bwd, gla_ba_bwd=gla_ba_bwd,
        gla_out_norm=gla_out_norm,
        mla_q_norm=mla_q_norm, mla_w_uq=mla_w_uq, mla_kv_norm=mla_kv_norm, mla_w_ukv=mla_w_ukv,
        mla_qk_q_norm=mla_qk_q_norm, mla_qk_k_norm=mla_qk_k_norm,
        w_out=w_out,
        xattn_norm=xattn_norm, mem_norm=mem_norm, xattn_wq=xattn_wq, xattn_wk=xattn_wk, xattn_wv=xattn_wv,
        xattn_q_norm=xattn_q_norm, xattn_k_norm=xattn_k_norm, xattn_wo=xattn_wo,
        ffn2_norm=ffn2_norm, ffn2_w_gate=ffn2_w_gate, ffn2_w_up=ffn2_w_up, ffn2_w_down=ffn2_w_down,
    )
    y_prompt, y_sample = x_prompt, x_sample
    for l in range(ffn1_norm.shape[0]):
        P = {name: a[l] for name, a in stacked.items()}
        W = _prepare(P)
        y_prompt, late = _layer(y_prompt, mem_prompt, W, P)
        y_sample, _ = _layer(y_sample, mem_sample, W, P, late)
    return (y_prompt, y_sample)
```
